```python
import math
import jax, jax.numpy as jnp
from jax import lax
import numpy as np

D_MODEL = 1024
BATCH = 4
SEQ = 8192
DEPTH = 1

HEAD_DIM = 64
N_ATT_HEADS = 12
ATT_WIDTH = N_ATT_HEADS * HEAD_DIM
DILATED_PATTERNS = ((128, 1), (512, 4), (2048, 16))
ATT_BLOCK = 128
SSM_EXPAND = 2
SSM_INNER = SSM_EXPAND * D_MODEL
SSM_HEAD_DIM = 64
SSM_HEADS = SSM_INNER // SSM_HEAD_DIM
SSM_GROUPS = 8
SSM_STATE = 128
SSM_CONV = 4
SSM_CHUNK = 128
CONV_DIM = SSM_INNER + 2 * SSM_GROUPS * SSM_STATE
FFN_HIDDEN = 4 * D_MODEL
N_BRANCHES = 2
IN_SPLITS = (ATT_WIDTH, ATT_WIDTH, ATT_WIDTH, SSM_INNER, CONV_DIM, SSM_HEADS, N_BRANCHES * D_MODEL)
IN_PROJ_WIDTH = sum(IN_SPLITS)
RMS_EPS = 1e-6

kernel_name = "hybrid_dilated_attn_mamba2_gated_block"


def rmsnorm(x, w):
    x32 = x.astype(jnp.float32)
    y = x32 * lax.rsqrt(jnp.mean(x32 * x32, axis=-1, keepdims=True) + RMS_EPS)
    return (y * w.astype(jnp.float32)).astype(x.dtype)


def alibi_slopes(n):
    def pow2(m):
        start = 2.0 ** (-8.0 / m)
        return [start ** (i + 1) for i in range(m)]
    if (n & (n - 1)) == 0:
        s = pow2(n)
    else:
        c = 2 ** int(math.floor(math.log2(n)))
        s = pow2(c) + pow2(2 * c)[0::2][: n - c]
    return jnp.asarray(np.array(s, dtype=np.float32))


def dilated_window_attention(q, k, v, slopes, window, dilation):
    b, s, h, dh = q.shape
    band = window // dilation
    blk = ATT_BLOCK
    span = dilation * blk
    s_pad = -(-s // span) * span
    pad = ((0, 0), (0, s_pad - s), (0, 0), (0, 0))
    q, k, v = [jnp.pad(t, pad) for t in (q, k, v)]
    nb = s_pad // span
    qb = q.reshape(b, nb, blk, dilation, h, dh)
    kb = k.reshape(b, nb, blk, dilation, h, dh)
    vb = v.reshape(b, nb, blk, dilation, h, dh)

    def with_prev(t):
        prev = jnp.concatenate([jnp.zeros_like(t[:, :1]), t[:, :-1]], axis=1)
        return jnp.concatenate([prev, t], axis=2)

    kk, vv = with_prev(kb), with_prev(vb)
    scores = jnp.einsum('bnirhd,bnjrhd->bnrhij', qb, kk,
                        preferred_element_type=jnp.float32) * (dh ** -0.5)
    i_idx = jnp.arange(blk)[:, None]
    j_idx = jnp.arange(2 * blk)[None, :]
    dist = blk + i_idx - j_idx
    n_idx = jnp.arange(nb)[:, None, None]
    valid = (dist >= 0) & (dist <= band) & ((n_idx > 0) | (j_idx >= blk))
    bias = -slopes[:, None, None] * (dist * dilation).astype(jnp.float32)
    scores = scores + bias[None, None, None]
    scores = jnp.where(valid[None, :, None, None], scores, -jnp.inf)
    m = jnp.max(scores, axis=-1, keepdims=True)
    p = jnp.exp(scores - m)
    l = jnp.sum(p, axis=-1)
    o = jnp.einsum('bnrhij,bnjrhd->bnirhd', p, vv.astype(jnp.float32))
    l_t = jnp.transpose(l, (0, 1, 4, 2, 3))
    m_t = jnp.transpose(m[..., 0], (0, 1, 4, 2, 3))
    o = o / l_t[..., None]
    o = o.reshape(b, s_pad, h, dh)[:, :s]
    return o, m_t.reshape(b, s_pad, h)[:, :s], l_t.reshape(b, s_pad, h)[:, :s]


def ssd_chunked(x, dt, a, bmat, cmat):
    b, s, g, hg, p = x.shape
    n = bmat.shape[-1]
    ch = SSM_CHUNK
    s_pad = -(-s // ch) * ch
    padlen = s_pad - s
    x = jnp.pad(x.astype(jnp.float32), ((0, 0), (0, padlen), (0, 0), (0, 0), (0, 0)))
    dt = jnp.pad(dt, ((0, 0), (0, padlen), (0, 0), (0, 0)))
    bmat = jnp.pad(bmat.astype(jnp.float32), ((0, 0), (0, padlen), (0, 0), (0, 0)))
    cmat = jnp.pad(cmat.astype(jnp.float32), ((0, 0), (0, padlen), (0, 0), (0, 0)))
    c = s_pad // ch
    xd = (x * dt[..., None]).reshape(b, c, ch, g, hg, p)
    la = jnp.moveaxis((dt * a).reshape(b, c, ch, g, hg), 2, -1)
    bc = bmat.reshape(b, c, ch, g, n)
    cc = cmat.reshape(b, c, ch, g, n)
    a_cs = jnp.cumsum(la, axis=-1)
    tril = jnp.tril(jnp.ones((ch, ch), dtype=bool))
    decay = jnp.exp(jnp.where(tril, a_cs[..., :, None] - a_cs[..., None, :], -jnp.inf))
    cb = jnp.einsum('bclgn,bcsgn->bcgls', cc, bc)
    y_diag = jnp.einsum('bcghls,bcsghp->bclghp', cb[:, :, :, None] * decay, xd)
    decay_states = jnp.exp(a_cs[..., -1:] - a_cs)
    states = jnp.einsum('bclgn,bcghl,bclghp->bcghpn', bc, decay_states, xd)
    chunk_decay = jnp.exp(a_cs[..., -1])

    def step(hstate, inp):
        st, dec = inp
        return hstate * dec[..., None, None] + st, hstate

    init = jnp.zeros((b, g, hg, p, n), jnp.float32)
    _, prev = lax.scan(step, init, (jnp.moveaxis(states, 1, 0), jnp.moveaxis(chunk_decay, 1, 0)))
    prev = jnp.moveaxis(prev, 0, 1)
    y_off = jnp.einsum('bclgn,bcghpn,bcghl->bclghp', cc, prev, jnp.exp(a_cs))
    return (y_diag + y_off).reshape(b, s_pad, g, hg, p)[:, :s]


def mamba2_mixer(z, xbc, dt_raw, conv_w, conv_b, dt_bias, a_log, d_skip, norm_w):
    b, s, _ = z.shape
    hg = SSM_HEADS // SSM_GROUPS
    xbc = lax.conv_general_dilated(xbc, conv_w[:, None, :], window_strides=(1,),
                                   padding=[(SSM_CONV - 1, 0)],
                                   dimension_numbers=('NWC', 'WIO', 'NWC'),
                                   feature_group_count=CONV_DIM) + conv_b
    xbc = jax.nn.silu(xbc)
    xs, bm, cm = jnp.split(xbc, [SSM_INNER, SSM_INNER + SSM_GROUPS * SSM_STATE], axis=-1)
    xh = xs.reshape(b, s, SSM_GROUPS, hg, SSM_HEAD_DIM)
    bm = bm.reshape(b, s, SSM_GROUPS, SSM_STATE)
    cm = cm.reshape(b, s, SSM_GROUPS, SSM_STATE)
    dt = jax.nn.softplus((dt_raw + dt_bias).astype(jnp.float32)).reshape(b, s, SSM_GROUPS, hg)
    a = -jnp.exp(a_log.astype(jnp.float32)).reshape(SSM_GROUPS, hg)
    y = ssd_chunked(xh, dt, a, bm, cm)
    y = y + d_skip.astype(jnp.float32).reshape(SSM_GROUPS, hg)[:, :, None] * xh.astype(jnp.float32)
    y = y.reshape(b, s, SSM_INNER) * jax.nn.silu(z.astype(jnp.float32))
    yg = y.reshape(b, s, SSM_GROUPS, SSM_INNER // SSM_GROUPS)
    yg = yg * lax.rsqrt(jnp.mean(yg * yg, axis=-1, keepdims=True) + RMS_EPS)
    y = yg.reshape(b, s, SSM_INNER) * norm_w.astype(jnp.float32)
    return y.astype(z.dtype)


def setup_inputs(seed: int = 0) -> dict:
    key = jax.random.key(seed)
    ks = jax.random.split(key, 20)
    f32 = jnp.float32

    def nrm(k, shape, scale):
        return jax.random.normal(k, shape, f32) * scale

    def gain(k, width):
        return 1.0 + 0.05 * jax.random.normal(k, (DEPTH, width), f32)

    dt0 = jnp.exp(jax.random.uniform(ks[7], (DEPTH, SSM_HEADS), f32,
                                     math.log(1e-3), math.log(1e-1)))
    return {
        "x": jax.random.normal(ks[0], (BATCH, SEQ, D_MODEL), f32),
        "norm_mix_pre_w": gain(ks[1], D_MODEL),
        "w_in": nrm(ks[2], (DEPTH, D_MODEL, IN_PROJ_WIDTH), D_MODEL ** -0.5),
        "b_gate": nrm(ks[3], (DEPTH, N_BRANCHES * D_MODEL), 0.02),
        "conv_w": nrm(ks[4], (DEPTH, SSM_CONV, CONV_DIM), SSM_CONV ** -0.5),
        "conv_b": nrm(ks[5], (DEPTH, CONV_DIM), 0.02),
        "dt_bias": dt0 + jnp.log(-jnp.expm1(-dt0)),
        "a_log": jnp.log(jax.random.uniform(ks[8], (DEPTH, SSM_HEADS), f32, 1.0, 16.0)),
        "d_skip": 1.0 + 0.1 * jax.random.normal(ks[9], (DEPTH, SSM_HEADS), f32),
        "ssm_norm_w": gain(ks[10], SSM_INNER),
        "w_att_proj": nrm(ks[11], (DEPTH, ATT_WIDTH, D_MODEL), ATT_WIDTH ** -0.5),
        "w_ssm_proj": nrm(ks[12], (DEPTH, SSM_INNER, D_MODEL), SSM_INNER ** -0.5),
        "w_out": nrm(ks[13], (DEPTH, D_MODEL, D_MODEL), D_MODEL ** -0.5),
        "norm_mix_post_w": gain(ks[14], D_MODEL),
        "norm_ffn_pre_w": gain(ks[15], D_MODEL),
        "w_up": nrm(ks[16], (DEPTH, D_MODEL, FFN_HIDDEN), D_MODEL ** -0.5),
        "w_down": nrm(ks[17], (DEPTH, FFN_HIDDEN, D_MODEL), FFN_HIDDEN ** -0.5),
        "norm_ffn_post_w": gain(ks[18], D_MODEL),
    }


def reference(x, norm_mix_pre_w, w_in, b_gate, conv_w, conv_b, dt_bias, a_log, d_skip,
              ssm_norm_w, w_att_proj, w_ssm_proj, w_out, norm_mix_post_w, norm_ffn_pre_w,
              w_up, w_down, norm_ffn_post_w):
    b, s, _ = x.shape
    slopes = alibi_slopes(N_ATT_HEADS)
    offsets = [int(o) for o in np.cumsum(IN_SPLITS)[:-1]]
    h = x
    for layer in range(DEPTH):
        u = rmsnorm(h, norm_mix_pre_w[layer])
        proj = u @ w_in[layer]
        q, k, v, z, xbc, dt_raw, gate_logits = jnp.split(proj, offsets, axis=-1)
        q = q.reshape(b, s, N_ATT_HEADS, HEAD_DIM)
        k = k.reshape(b, s, N_ATT_HEADS, HEAD_DIM)
        v = v.reshape(b, s, N_ATT_HEADS, HEAD_DIM)
        outs, maxes, dens = [], [], []
        for window, dilation in DILATED_PATTERNS:
            o_g, m_g, l_g = dilated_window_attention(q, k, v, slopes, window, dilation)
            outs.append(o_g)
            maxes.append(m_g)
            dens.append(l_g)
        m_all = jnp.stack(maxes)
        wts = jnp.exp(m_all - jnp.max(m_all, axis=0, keepdims=True)) * jnp.stack(dens)
        att = jnp.sum(wts[..., None] * jnp.stack(outs), axis=0) / jnp.sum(wts, axis=0)[..., None]
        att = att.reshape(b, s, ATT_WIDTH).astype(x.dtype) @ w_att_proj[layer]
        ssm = mamba2_mixer(z, xbc, dt_raw, conv_w[layer], conv_b[layer], dt_bias[layer],
                           a_log[layer], d_skip[layer], ssm_norm_w[layer])
        ssm = ssm @ w_ssm_proj[layer]
        gates = jax.nn.sigmoid(gate_logits + b_gate[layer])
        g_att, g_ssm = jnp.split(gates, 2, axis=-1)
        mixed = (g_att * att + g_ssm * ssm) @ w_out[layer]
        h = h + rmsnorm(mixed, norm_mix_post_w[layer])
        f = rmsnorm(h, norm_ffn_pre_w[layer])
        f = jnp.square(jax.nn.relu(f @ w_up[layer])) @ w_down[layer]
        h = h + rmsnorm(f, norm_ffn_post_w[layer])
    return h
```

```python
import functools
import math

import numpy as np
import jax
import jax.numpy as jnp
from jax import lax
from jax.experimental import pallas as pl
from jax.experimental.pallas import tpu as pltpu

F32 = jnp.float32
BF16 = jnp.bfloat16

D_MODEL = 1024
HEAD_DIM = 64
N_ATT_HEADS = 12
ATT_WIDTH = N_ATT_HEADS * HEAD_DIM
DILATIONS = (1, 4, 16)
ATT_BLOCK = 128
SSM_INNER = 2048
SSM_HEADS = 32
SSM_GROUPS = 8
SSM_STATE = 128
SSM_CONV = 4
SSM_CHUNK = 128
CONV_DIM = SSM_INNER + 2 * SSM_GROUPS * SSM_STATE
FFN_HIDDEN = 4 * D_MODEL
RMS_EPS = 1e-6
NEG = -1e30

LANES = 128
HEADS_PER_TILE = LANES // HEAD_DIM
N_HEAD_TILES = N_ATT_HEADS // HEADS_PER_TILE
GROUP_WIDTH = SSM_INNER // SSM_GROUPS
HEADS_PER_GROUP = SSM_HEADS // SSM_GROUPS
VMEM_LIMIT_BYTES = 56 * 1024 * 1024


def _const_spec(shape):
    return pl.BlockSpec(shape, lambda *_: (0,) * len(shape), pipeline_mode=pl.Buffered(1))


def _params(n_axes):
    return pltpu.CompilerParams(dimension_semantics=("arbitrary",) * n_axes,
                                vmem_limit_bytes=VMEM_LIMIT_BYTES)


def _rms_scale(x):
    return lax.rsqrt(jnp.mean(x * x, axis=-1, keepdims=True) + RMS_EPS)


IN_TM = 512
_OFF_Q, _OFF_K, _OFF_V = 0, ATT_WIDTH, 2 * ATT_WIDTH
_OFF_Z = 3 * ATT_WIDTH
_OFF_XBC = _OFF_Z + SSM_INNER
_OFF_GATE = _OFF_XBC + CONV_DIM
_OFF_DT = _OFF_GATE + 2 * D_MODEL
_W_ALL = _OFF_DT + LANES


def _in_proj_kernel(x_ref, nw_ref, w_ref, bg_ref, dtb_ref,
                    q_ref, k_ref, v_ref, z_ref, xbc_ref, gate_ref, dt_ref, u_ref):
    x = x_ref[...]
    u_ref[...] = (x * _rms_scale(x) * nw_ref[...]).astype(BF16)

    def proj(off, width):
        return jnp.dot(u_ref[...], w_ref[:, off:off + width], preferred_element_type=F32)

    chunk = 256
    for out_ref, off, width in ((q_ref, _OFF_Q, ATT_WIDTH), (k_ref, _OFF_K, ATT_WIDTH),
                                (v_ref, _OFF_V, ATT_WIDTH), (z_ref, _OFF_Z, SSM_INNER),
                                (xbc_ref, _OFF_XBC, CONV_DIM)):
        for c in range(0, width, chunk):
            out_ref[:, c:c + chunk] = proj(off + c, chunk).astype(BF16)
    for c in range(0, 2 * D_MODEL, chunk):
        logits = proj(_OFF_GATE + c, chunk) + bg_ref[:, c:c + chunk]
        gate_ref[:, c:c + chunk] = jax.nn.sigmoid(logits).astype(BF16)
    dt_raw = proj(_OFF_DT, LANES) + dtb_ref[...]
    dt_ref[...] = jnp.maximum(dt_raw, 0.0) + jnp.log1p(jnp.exp(-jnp.abs(dt_raw)))


def _in_proj(x2, norm_w, w_all, b_gate, dt_bias_pad):
    t = x2.shape[0]
    tm = IN_TM
    row = lambda width: pl.BlockSpec((tm, width), lambda i: (i, 0))
    widths = (ATT_WIDTH, ATT_WIDTH, ATT_WIDTH, SSM_INNER, CONV_DIM, 2 * D_MODEL)
    out_shape = [jax.ShapeDtypeStruct((t, w), BF16) for w in widths]
    out_shape.append(jax.ShapeDtypeStruct((t, LANES), F32))
    return pl.pallas_call(
        _in_proj_kernel,
        grid=(t // tm,),
        in_specs=[row(D_MODEL), _const_spec((1, D_MODEL)), _const_spec((D_MODEL, _W_ALL)),
                  _const_spec((1, 2 * D_MODEL)), _const_spec((1, LANES))],
        out_specs=[row(w) for w in widths] + [row(LANES)],
        out_shape=out_shape,
        scratch_shapes=[pltpu.VMEM((tm, D_MODEL), BF16)],
        compiler_params=_params(1),
        name="in_proj",
    )(x2, norm_w, w_all, b_gate, dt_bias_pad)


ATT_SUB = 4


def _alibi_slopes(n):
    def pow2(m):
        start = 2.0 ** (-8.0 / m)
        return [start ** (i + 1) for i in range(m)]
    if (n & (n - 1)) == 0:
        s = pow2(n)
    else:
        c = 2 ** int(math.floor(math.log2(n)))
        s = pow2(c) + pow2(2 * c)[0::2][: n - c]
    return np.array(s, dtype=np.float32)


def _attention_bias(dilation):
    blk = ATT_BLOCK
    i = np.arange(blk)[:, None]
    j = np.arange(blk)[None, :]
    dist_prev = blk + i - j
    dist_cur = i - j
    slopes = _alibi_slopes(N_ATT_HEADS).reshape(N_HEAD_TILES, HEADS_PER_TILE)
    out = np.empty((2, N_HEAD_TILES, HEADS_PER_TILE * blk, 2 * blk), np.float32)
    for tile in range(N_HEAD_TILES):
        for hh in range(HEADS_PER_TILE):
            s = slopes[tile, hh]
            prev = np.where(dist_prev <= blk, -s * (dist_prev * dilation).astype(np.float32), NEG)
            cur = np.where(dist_cur >= 0, -s * (dist_cur * dilation).astype(np.float32), NEG)
            rows = slice(hh * blk, (hh + 1) * blk)
            out[1, tile, rows, :blk] = prev
            out[0, tile, rows, :blk] = NEG
            out[:, tile, rows, blk:] = cur
    return jnp.asarray(out)


def _attention_kernel(q_ref, kc_ref, kp_ref, vc_ref, vp_ref, bias_ref, o_ref, lse_ref, *, dilation):
    blk = ATT_BLOCK
    n = pl.program_id(1)
    lane = lax.broadcasted_iota(jnp.int32, (blk, LANES), 1)
    in_head_a = lane < HEAD_DIM
    scale_a = jnp.where(in_head_a, HEAD_DIM ** -0.5, 0.0).astype(BF16)
    scale_b = jnp.where(in_head_a, 0.0, HEAD_DIM ** -0.5).astype(BF16)
    ones = jnp.ones((2 * blk, LANES), BF16)

    for g in range(ATT_SUB):
        if dilation == 1:
            rows, col0 = slice(g * blk, (g + 1) * blk), 0
            prev_rows = slice((g - 1) * blk, g * blk) if g > 0 else slice(0, blk)
            has_prev = (n > 0) if g == 0 else None
        else:
            rows, col0 = slice(0, blk), g * ATT_WIDTH
            prev_rows = rows
            has_prev = n > 0
        stats = jnp.zeros((blk, LANES), F32)
        for tile in range(N_HEAD_TILES):
            cols = slice(col0 + tile * LANES, col0 + (tile + 1) * LANES)
            q = q_ref[0, rows, cols]
            k_cur, v_cur = kc_ref[0, rows, cols], vc_ref[0, rows, cols]
            if dilation == 1 and g > 0:
                k_prev, v_prev = kc_ref[0, prev_rows, cols], vc_ref[0, prev_rows, cols]
            else:
                k_prev, v_prev = kp_ref[0, prev_rows, cols], vp_ref[0, prev_rows, cols]
            q2 = jnp.concatenate([q * scale_a, q * scale_b], axis=0)
            k2 = jnp.concatenate([k_prev, k_cur], axis=0)
            s = lax.dot_general(q2, k2, (((1,), (1,)), ((), ())), preferred_element_type=F32)
            if has_prev is None:
                s = s + bias_ref[1, tile]
            else:
                s = s + bias_ref[has_prev.astype(jnp.int32), tile]
            m = jnp.max(s, axis=-1, keepdims=True)
            p = jnp.exp(s - m).astype(BF16)
            v2 = jnp.concatenate([jnp.concatenate([v_prev, v_cur], axis=0), ones], axis=1)
            r = jnp.dot(p, v2, preferred_element_type=F32)
            acc, den = r[:, :LANES], r[:, LANES:]
            o = acc / den
            o_ref[0, rows, cols] = jnp.where(in_head_a, o[:blk], o[blk:]).astype(BF16)
            lse = m + jnp.log(den)
            stats = jnp.where(lane == tile * HEADS_PER_TILE, lse[:blk], stats)
            stats = jnp.where(lane == tile * HEADS_PER_TILE + 1, lse[blk:], stats)
        if dilation == 1:
            lse_ref[0, rows, :] = stats
        else:
            lse_ref[0, :, g * LANES:(g + 1) * LANES] = stats


def _attention(q, k, v, dilation):
    b, s, _ = q.shape
    blk = ATT_BLOCK
    d = dilation
    view = lambda a: a.reshape(b, s // d, d * a.shape[-1])
    if d == 1:
        grid = (b, s // (blk * ATT_SUB))
        cur = lambda w: pl.BlockSpec((1, blk * ATT_SUB, w), lambda bi, n: (bi, n, 0))
        prev = lambda w: pl.BlockSpec((1, blk, w), lambda bi, n: (bi, jnp.maximum(ATT_SUB * n - 1, 0), 0))
    else:
        groups = d // ATT_SUB
        grid = (b, s // (blk * d), groups)
        cur = lambda w: pl.BlockSpec((1, blk, ATT_SUB * w), lambda bi, n, rg: (bi, n, rg))
        prev = lambda w: pl.BlockSpec((1, blk, ATT_SUB * w),
                                      lambda bi, n, rg: (bi, jnp.maximum(n - 1, 0), rg))
    bias = _attention_bias(d)
    o, lse = pl.pallas_call(
        functools.partial(_attention_kernel, dilation=d),
        grid=grid,
        in_specs=[cur(ATT_WIDTH), cur(ATT_WIDTH), prev(ATT_WIDTH), cur(ATT_WIDTH), prev(ATT_WIDTH),
                  _const_spec(bias.shape)],
        out_specs=[cur(ATT_WIDTH), cur(LANES)],
        out_shape=[jax.ShapeDtypeStruct((b, s // d, d * ATT_WIDTH), BF16),
                   jax.ShapeDtypeStruct((b, s // d, d * LANES), F32)],
        compiler_params=_params(len(grid)),
        name=f"attention_d{d}",
    )(view(q), view(k), view(k), view(v), view(v), bias)
    return o.reshape(b, s, ATT_WIDTH), lse.reshape(b, s, LANES)


_HALO = 8


def _split3(x):
    hi = x.astype(BF16)
    r = x - hi.astype(F32)
    mid = r.astype(BF16)
    lo = (r - mid.astype(F32)).astype(BF16)
    return hi, mid, lo


def _ssd_kernel(xbc_ref, z_ref, dt_ref, cw_ref, cb_ref, alog_ref, dskip_ref, nw_ref,
                y_ref, xpad_ref, xc_ref, st_ref):
    ch = SSM_CHUNK
    c = pl.program_id(1)

    @pl.when(c == 0)
    def _():
        st_ref[...] = jnp.zeros_like(st_ref)
        xpad_ref[0:_HALO, :] = jnp.zeros((_HALO, CONV_DIM), F32)

    xpad_ref[_HALO:_HALO + ch, :] = xbc_ref[0].astype(F32)
    lane_chunk = 512
    for c0 in range(0, CONV_DIM, lane_chunk):
        cs = slice(c0, c0 + lane_chunk)
        acc = jnp.broadcast_to(cb_ref[:, cs], (ch, lane_chunk))
        for tap in range(SSM_CONV):
            start = _HALO - (SSM_CONV - 1) + tap
            acc = acc + cw_ref[tap:tap + 1, cs] * xpad_ref[start:start + ch, cs]
        xc_ref[:, cs] = acc * jax.nn.sigmoid(acc)
    xpad_ref[0:_HALO, :] = xpad_ref[ch:ch + _HALO, :]

    dt = dt_ref[0]
    la = dt * (-jnp.exp(alog_ref[...]))
    row_i = lax.broadcasted_iota(jnp.int32, (ch, ch), 0)
    col_i = lax.broadcasted_iota(jnp.int32, (ch, ch), 1)
    causal = row_i >= col_i
    tri = jnp.where(causal, 1.0, 0.0).astype(BF16)
    a_cs = sum(jnp.dot(tri, part, preferred_element_type=F32) for part in _split3(la))
    a_cs_t = a_cs.T
    dt_t = dt.T
    w_t = jnp.exp(a_cs_t[:, ch - 1:ch] - a_cs_t) * dt_t
    lane2 = lax.broadcasted_iota(jnp.int32, (ch, GROUP_WIDTH), 1)
    lane1 = lax.broadcasted_iota(jnp.int32, (ch, LANES), 1)

    for g in range(SSM_GROUPS):
        gs = slice(g * GROUP_WIDTH, (g + 1) * GROUP_WIDTH)
        b_g = xc_ref[:, SSM_INNER + g * SSM_STATE:SSM_INNER + (g + 1) * SSM_STATE]
        c_g = xc_ref[:, SSM_INNER + (SSM_GROUPS + g) * SSM_STATE:
                     SSM_INNER + (SSM_GROUPS + g + 1) * SSM_STATE]
        b_bf, c_bf = b_g.astype(BF16), c_g.astype(BF16)
        cb = lax.dot_general(c_bf, b_bf, (((1,), (1,)), ((), ())), preferred_element_type=F32)
        b_t = b_g.T
        x_g = xc_ref[:, gs]
        x_bf = x_g.astype(BF16)

        intra, to_state, a_cols, x_diag = [], [], [], []
        for hh in range(HEADS_PER_GROUP):
            h = g * HEADS_PER_GROUP + hh
            a_col = jnp.broadcast_to(a_cs[:, h:h + 1], (ch, ch))
            a_row = a_cs_t[h:h + 1, :]
            decay = jnp.exp(jnp.where(causal, a_col - a_row, NEG))
            intra.append((cb * decay * dt_t[h:h + 1, :]).astype(BF16))
            to_state.append((b_t * w_t[h:h + 1, :]).astype(BF16))
            a_cols.append(a_col)
            in_head = (lane2 >= hh * HEAD_DIM) & (lane2 < (hh + 1) * HEAD_DIM)
            x_diag.append(jnp.where(in_head, x_bf, jnp.zeros_like(x_bf)))
        lhs = jnp.concatenate([jnp.concatenate(intra, axis=1), jnp.concatenate(to_state, axis=1)], axis=0)
        r = jnp.dot(lhs, jnp.concatenate(x_diag, axis=0), preferred_element_type=F32)
        y_diag, st_inc = r[:ch], r[ch:]

        first = lane1 < HEAD_DIM
        a_exp = jnp.exp(jnp.concatenate([jnp.where(first, a_cols[0], a_cols[1]),
                                         jnp.where(first, a_cols[2], a_cols[3])], axis=1))
        st_old = st_ref[g]
        y_off = jnp.dot(c_bf, st_old.astype(BF16), preferred_element_type=F32) * a_exp
        st_ref[g] = st_old * a_exp[ch - 1:ch, :] + st_inc

        y = y_diag + y_off + dskip_ref[:, gs] * x_g
        zg = z_ref[0, :, gs].astype(F32)
        y = y * (zg * jax.nn.sigmoid(zg))
        y_ref[0, :, gs] = (y * _rms_scale(y) * nw_ref[:, gs]).astype(BF16)


def _ssd(xbc, z, dt, conv_w, conv_b, a_log_pad, d_skip_wide, norm_w):
    b, s, _ = xbc.shape
    ch = SSM_CHUNK
    blk = lambda w: pl.BlockSpec((1, ch, w), lambda bi, c: (bi, c, 0))
    return pl.pallas_call(
        _ssd_kernel,
        grid=(b, s // ch),
        in_specs=[blk(CONV_DIM), blk(SSM_INNER), blk(LANES),
                  _const_spec((SSM_CONV, CONV_DIM)), _const_spec((1, CONV_DIM)),
                  _const_spec((1, LANES)), _const_spec((1, SSM_INNER)), _const_spec((1, SSM_INNER))],
        out_specs=blk(SSM_INNER),
        out_shape=jax.ShapeDtypeStruct((b, s, SSM_INNER), BF16),
        scratch_shapes=[pltpu.VMEM((_HALO + ch, CONV_DIM), F32),
                        pltpu.VMEM((ch, CONV_DIM), F32),
                        pltpu.VMEM((SSM_GROUPS, SSM_STATE, GROUP_WIDTH), F32)],
        compiler_params=_params(2),
        name="ssd",
    )(xbc, z, dt, conv_w, conv_b, a_log_pad, d_skip_wide, norm_w)


MERGE_TM = 512


def _merge_kernel(x_ref, o1_ref, o4_ref, o16_ref, l1_ref, l4_ref, l16_ref, y_ref, gate_ref,
                  expand_ref, watt_ref, wssm_ref, wout_ref, nw_ref, h_ref):
    lses = [l1_ref[...], l4_ref[...], l16_ref[...]]
    top = jnp.maximum(jnp.maximum(lses[0], lses[1]), lses[2])
    wts = [jnp.exp(l - top) for l in lses]
    inv = 1.0 / (wts[0] + wts[1] + wts[2])
    att = jnp.zeros((MERGE_TM, ATT_WIDTH), F32)
    for w, o_ref in zip(wts, (o1_ref, o4_ref, o16_ref)):
        wide = jnp.dot((w * inv).astype(BF16), expand_ref[...], preferred_element_type=F32)
        att = att + wide * o_ref[...].astype(F32)
    att = jnp.dot(att.astype(BF16), watt_ref[...], preferred_element_type=F32)
    ssm = jnp.dot(y_ref[...], wssm_ref[...], preferred_element_type=F32)
    gates = gate_ref[...].astype(F32)
    mixed = gates[:, :D_MODEL] * att + gates[:, D_MODEL:] * ssm
    mixed = jnp.dot(mixed.astype(BF16), wout_ref[...], preferred_element_type=F32)
    h_ref[...] = x_ref[...] + mixed * _rms_scale(mixed) * nw_ref[...]


def _merge(x2, outs, lses, y, gates, expand, w_att, w_ssm, w_out, norm_w):
    t = x2.shape[0]
    tm = MERGE_TM
    row = lambda width: pl.BlockSpec((tm, width), lambda i: (i, 0))
    return pl.pallas_call(
        _merge_kernel,
        grid=(t // tm,),
        in_specs=[row(D_MODEL)] + [row(ATT_WIDTH)] * 3 + [row(LANES)] * 3
                 + [row(SSM_INNER), row(2 * D_MODEL),
                    _const_spec((LANES, ATT_WIDTH)), _const_spec((ATT_WIDTH, D_MODEL)),
                    _const_spec((SSM_INNER, D_MODEL)), _const_spec((D_MODEL, D_MODEL)),
                    _const_spec((1, D_MODEL))],
        out_specs=row(D_MODEL),
        out_shape=jax.ShapeDtypeStruct((t, D_MODEL), F32),
        compiler_params=_params(1),
        name="merge",
    )(x2, *outs, *lses, y, gates, expand, w_att, w_ssm, w_out, norm_w)


FFN_TM = 512
FFN_CHUNK = 1024


def _ffn_kernel(h_ref, pre_ref, wup_ref, wdown_ref, post_ref, o_ref):
    h = h_ref[...]
    f = (h * _rms_scale(h) * pre_ref[...]).astype(BF16)
    acc = jnp.zeros((FFN_TM, D_MODEL), F32)
    for c in range(0, FFN_HIDDEN, FFN_CHUNK):
        up = jnp.dot(f, wup_ref[:, c:c + FFN_CHUNK], preferred_element_type=F32)
        act = jnp.square(jnp.maximum(up, 0.0)).astype(BF16)
        acc = acc + jnp.dot(act, wdown_ref[c:c + FFN_CHUNK, :], preferred_element_type=F32)
    o_ref[...] = h + acc * _rms_scale(acc) * post_ref[...]


def _ffn(h2, pre_w, w_up, w_down, post_w):
    t = h2.shape[0]
    tm = FFN_TM
    row = pl.BlockSpec((tm, D_MODEL), lambda i: (i, 0))
    return pl.pallas_call(
        _ffn_kernel,
        grid=(t // tm,),
        in_specs=[row, _const_spec((1, D_MODEL)), _const_spec((D_MODEL, FFN_HIDDEN)),
                  _const_spec((FFN_HIDDEN, D_MODEL)), _const_spec((1, D_MODEL))],
        out_specs=row,
        out_shape=jax.ShapeDtypeStruct((t, D_MODEL), F32),
        compiler_params=_params(1),
        name="ffn",
    )(h2, pre_w, w_up, w_down, post_w)


def _pad_lanes(v):
    return jnp.pad(v.astype(F32), (0, LANES - v.shape[0])).reshape(1, LANES)


def _head_expand_matrix():
    e = np.zeros((LANES, ATT_WIDTH), np.float32)
    for h in range(N_ATT_HEADS):
        e[h, h * HEAD_DIM:(h + 1) * HEAD_DIM] = 1.0
    return jnp.asarray(e, dtype=BF16)


def kernel(x, norm_mix_pre_w, w_in, b_gate, conv_w, conv_b, dt_bias, a_log, d_skip, ssm_norm_w,
           w_att_proj, w_ssm_proj, w_out, norm_mix_post_w, norm_ffn_pre_w, w_up, w_down,
           norm_ffn_post_w):
    b, s, d_model = x.shape
    depth = w_in.shape[0]
    assert d_model == D_MODEL
    assert s % (ATT_BLOCK * max(DILATIONS)) == 0 and (b * s) % IN_TM == 0
    t = b * s
    expand = _head_expand_matrix()
    h = x.reshape(t, D_MODEL)
    for layer in range(depth):
        w = w_in[layer]
        off_dt = _OFF_XBC + CONV_DIM
        w_all = jnp.concatenate(
            [w[:, :off_dt], w[:, off_dt + SSM_HEADS:], jnp.pad(w[:, off_dt:off_dt + SSM_HEADS],
                                                              ((0, 0), (0, LANES - SSM_HEADS)))],
            axis=1).astype(BF16)
        q, k, v, z, xbc, gates, dt = _in_proj(
            h, norm_mix_pre_w[layer].reshape(1, -1), w_all, b_gate[layer].reshape(1, -1),
            _pad_lanes(dt_bias[layer]))
        seq = lambda a: a.reshape(b, s, a.shape[-1])
        outs, lses = [], []
        for d in DILATIONS:
            o, lse = _attention(seq(q), seq(k), seq(v), d)
            outs.append(o.reshape(t, ATT_WIDTH))
            lses.append(lse.reshape(t, LANES))
        y = _ssd(seq(xbc), seq(z), seq(dt), conv_w[layer], conv_b[layer].reshape(1, -1),
                 _pad_lanes(a_log[layer]),
                 jnp.repeat(d_skip[layer].astype(F32), HEAD_DIM).reshape(1, -1),
                 ssm_norm_w[layer].reshape(1, -1))
        h = _merge(h, outs, lses, y.reshape(t, SSM_INNER), gates, expand,
                   w_att_proj[layer].astype(BF16), w_ssm_proj[layer].astype(BF16),
                   w_out[layer].astype(BF16), norm_mix_post_w[layer].reshape(1, -1))
        h = _ffn(h, norm_ffn_pre_w[layer].reshape(1, -1), w_up[layer].astype(BF16),
                 w_down[layer].astype(BF16), norm_ffn_post_w[layer].reshape(1, -1))
    return h.reshape(b, s, D_MODEL)
```

```python
import functools
import math

import numpy as np
import jax
import jax.numpy as jnp
from jax import lax
from jax.experimental import pallas as pl
from jax.experimental.pallas import tpu as pltpu

F32 = jnp.float32
BF16 = jnp.bfloat16

D_MODEL = 1024
HEAD_DIM = 64
N_ATT_HEADS = 12
ATT_WIDTH = N_ATT_HEADS * HEAD_DIM
DILATIONS = (1, 4, 16)
ATT_BLOCK = 128
SSM_INNER = 2048
SSM_HEADS = 32
SSM_GROUPS = 8
SSM_STATE = 128
SSM_CONV = 4
SSM_CHUNK = 128
CONV_DIM = SSM_INNER + 2 * SSM_GROUPS * SSM_STATE
FFN_HIDDEN = 4 * D_MODEL
RMS_EPS = 1e-6
NEG = -1e30

LANES = 128
MXU_N = 256
HEADS_PER_TILE = LANES // HEAD_DIM
N_HEAD_TILES = N_ATT_HEADS // HEADS_PER_TILE
GROUP_WIDTH = SSM_INNER // SSM_GROUPS
HEADS_PER_GROUP = SSM_HEADS // SSM_GROUPS
VMEM_LIMIT_BYTES = 56 * 1024 * 1024

ROW_TILE = 512
SPAN4 = ATT_BLOCK * 4
SPAN16 = ATT_BLOCK * 16
assert ROW_TILE == SPAN4


def _const_spec(shape):
    return pl.BlockSpec(shape, lambda *_: (0,) * len(shape), pipeline_mode=pl.Buffered(1))


def _params(n_axes):
    return pltpu.CompilerParams(dimension_semantics=("arbitrary",) * n_axes,
                                vmem_limit_bytes=VMEM_LIMIT_BYTES)


def _rms_scale(x):
    return lax.rsqrt(jnp.mean(x * x, axis=-1, keepdims=True) + RMS_EPS)


def _residue16_spec(width):
    per_span = SPAN16 // ROW_TILE
    return pl.BlockSpec((1, 16, ROW_TILE // 16, width), lambda i: (i // per_span, 0, i % per_span, 0))


def _residue4_spec(width):
    return pl.BlockSpec((1, 4, ATT_BLOCK, width), lambda i: (i, 0, 0, 0))


_OFF_Z = 0
_OFF_XBC = _OFF_Z + SSM_INNER
_OFF_GATE = _OFF_XBC + CONV_DIM
_OFF_DT = _OFF_GATE + 2 * D_MODEL
_W_SSM = _OFF_DT + LANES
_QKV_SLABS = 3 * ATT_WIDTH // LANES


def _qkv_kernel(x_ref, nw_ref, w_ref, *refs):
    out_refs, (u_ref, slab_ref, res4_ref) = refs[:9], refs[9:]
    tm = ROW_TILE
    x = x_ref[...]
    u_ref[...] = (x * _rms_scale(x) * nw_ref[...]).astype(BF16)
    for a in range(3):
        nat_ref, d4_ref, d16_ref = out_refs[3 * a:3 * a + 3]
        for c in range(0, ATT_WIDTH, MXU_N):
            acc = jnp.dot(u_ref[...], w_ref[:, a * ATT_WIDTH + c:a * ATT_WIDTH + c + MXU_N],
                          preferred_element_type=F32)
            nat_ref[:, c:c + MXU_N] = acc.astype(BF16)
            for half in range(MXU_N // LANES):
                cols = slice(c + half * LANES, c + (half + 1) * LANES)
                slab = (a * ATT_WIDTH + c) // LANES + half
                slab_ref[slab] = acc[:, half * LANES:(half + 1) * LANES]
                for r in range(4):
                    part = slab_ref[slab, pl.ds(r, tm // 4, stride=4), :]
                    d4_ref[0, r, :, cols] = part.astype(BF16)
                    res4_ref[slab, r] = part
                for r in range(16):
                    part = res4_ref[slab, r % 4, pl.ds(r // 4, tm // 16, stride=4), :]
                    d16_ref[0, r, :, cols] = part.astype(BF16)


def _qkv_proj(x2, norm_w, w_qkv):
    t = x2.shape[0]
    tm = ROW_TILE
    nat = pl.BlockSpec((tm, ATT_WIDTH), lambda i: (i, 0))
    shapes = [jax.ShapeDtypeStruct((t, ATT_WIDTH), BF16),
              jax.ShapeDtypeStruct((t // SPAN4, 4, ATT_BLOCK, ATT_WIDTH), BF16),
              jax.ShapeDtypeStruct((t // SPAN16, 16, ATT_BLOCK, ATT_WIDTH), BF16)]
    return pl.pallas_call(
        _qkv_kernel,
        grid=(t // tm,),
        in_specs=[pl.BlockSpec((tm, D_MODEL), lambda i: (i, 0)), _const_spec((1, D_MODEL)),
                  _const_spec((D_MODEL, 3 * ATT_WIDTH))],
        out_specs=[nat, _residue4_spec(ATT_WIDTH), _residue16_spec(ATT_WIDTH)] * 3,
        out_shape=shapes * 3,
        scratch_shapes=[pltpu.VMEM((tm, D_MODEL), BF16),
                        pltpu.VMEM((_QKV_SLABS, tm, LANES), F32),
                        pltpu.VMEM((_QKV_SLABS, 4, tm // 4, LANES), F32)],
        compiler_params=_params(1),
        name="qkv_proj",
    )(x2, norm_w, w_qkv)


_HALO = 8
_PAD_BUFFERS = 8


def _ssm_proj_kernel(x_ref, nw_ref, w_ref, bg_ref, dtb_ref, cw_ref, cb_ref,
                     zs_ref, xc_ref, gate_ref, dt_ref, u_ref, halo_ref, pad_ref, *, tiles_per_seq):
    tm = ROW_TILE
    x = x_ref[...]
    u_ref[...] = (x * _rms_scale(x) * nw_ref[...]).astype(BF16)

    @pl.when(pl.program_id(0) % tiles_per_seq == 0)
    def _():
        halo_ref[...] = jnp.zeros_like(halo_ref)

    def proj(off, width):
        return jnp.dot(u_ref[...], w_ref[:, off:off + width], preferred_element_type=F32)

    for c in range(0, SSM_INNER, MXU_N):
        h = proj(_OFF_Z + c, MXU_N)
        zs_ref[:, c:c + MXU_N] = (h + h * jnp.tanh(h)).astype(BF16)
    for idx, c in enumerate(range(0, CONV_DIM, LANES)):
        cs = slice(c, c + LANES)
        buf = idx % _PAD_BUFFERS
        if c % MXU_N == 0:
            acc2 = proj(_OFF_XBC + c, MXU_N)
        acc = acc2[:, c % MXU_N:c % MXU_N + LANES]
        pad_ref[buf, pl.ds(0, _HALO, stride=2), :] = halo_ref[:, cs]
        pad_ref[buf, pl.ds(2 * _HALO, tm, stride=2), :] = acc
        halo_ref[:, cs] = acc[tm - _HALO:, :]
        h = cb_ref[:, cs] + cw_ref[SSM_CONV - 1:SSM_CONV, cs] * acc
        for tap in range(SSM_CONV - 1):
            start = 2 * (_HALO - (SSM_CONV - 1) + tap)
            h = h + cw_ref[tap:tap + 1, cs] * pad_ref[buf, pl.ds(start, tm, stride=2), :]
        xc_ref[:, cs] = (h + h * jnp.tanh(h)).astype(BF16)
    for c in range(0, 2 * D_MODEL, MXU_N):
        logits = proj(_OFF_GATE + c, MXU_N) + bg_ref[:, c:c + MXU_N]
        gate_ref[:, c:c + MXU_N] = jax.nn.sigmoid(logits).astype(BF16)
    dt_raw = proj(_OFF_DT, LANES) + dtb_ref[...]
    dt_ref[...] = jnp.maximum(dt_raw, 0.0) + jnp.log1p(jnp.exp(-jnp.abs(dt_raw)))


def _ssm_proj(x2, norm_w, w_ssm, b_gate, dt_bias_pad, conv_w_half, conv_b_half, seq_len):
    t = x2.shape[0]
    tm = ROW_TILE
    row = lambda width: pl.BlockSpec((tm, width), lambda i: (i, 0))
    widths = (SSM_INNER, CONV_DIM, 2 * D_MODEL)
    out_shape = [jax.ShapeDtypeStruct((t, w), BF16) for w in widths]
    out_shape.append(jax.ShapeDtypeStruct((t, LANES), F32))
    return pl.pallas_call(
        functools.partial(_ssm_proj_kernel, tiles_per_seq=seq_len // tm),
        grid=(t // tm,),
        in_specs=[row(D_MODEL), _const_spec((1, D_MODEL)), _const_spec((D_MODEL, _W_SSM)),
                  _const_spec((1, 2 * D_MODEL)), _const_spec((1, LANES)),
                  _const_spec((SSM_CONV, CONV_DIM)), _const_spec((1, CONV_DIM))],
        out_specs=[row(w) for w in widths] + [row(LANES)],
        out_shape=out_shape,
        scratch_shapes=[pltpu.VMEM((tm, D_MODEL), BF16),
                        pltpu.VMEM((_HALO, CONV_DIM), F32),
                        pltpu.VMEM((_PAD_BUFFERS, 2 * (_HALO + tm), LANES), F32)],
        compiler_params=_params(1),
        name="ssm_proj",
    )(x2, norm_w, w_ssm, b_gate, dt_bias_pad, conv_w_half, conv_b_half)


ATT_SUB = 4


def _alibi_slopes(n):
    def pow2(m):
        start = 2.0 ** (-8.0 / m)
        return [start ** (i + 1) for i in range(m)]
    if (n & (n - 1)) == 0:
        s = pow2(n)
    else:
        c = 2 ** int(math.floor(math.log2(n)))
        s = pow2(c) + pow2(2 * c)[0::2][: n - c]
    return np.array(s, dtype=np.float32)


def _attention_bias(dilation):
    blk = ATT_BLOCK
    i = np.arange(blk)[:, None]
    j = np.arange(blk)[None, :]
    dist_prev = blk + i - j
    dist_cur = i - j
    slopes = _alibi_slopes(N_ATT_HEADS).reshape(N_HEAD_TILES, HEADS_PER_TILE)
    out = np.empty((2, N_HEAD_TILES, HEADS_PER_TILE * blk, 2 * blk), np.float32)
    for tile in range(N_HEAD_TILES):
        for hh in range(HEADS_PER_TILE):
            s = slopes[tile, hh]
            prev = np.where(dist_prev <= blk, -s * (dist_prev * dilation).astype(np.float32), NEG)
            cur = np.where(dist_cur >= 0, -s * (dist_cur * dilation).astype(np.float32), NEG)
            rows = slice(hh * blk, (hh + 1) * blk)
            out[1, tile, rows, :blk] = prev
            out[0, tile, rows, :blk] = NEG
            out[:, tile, rows, blk:] = cur
    return jnp.asarray(out)


def _attention_kernel(q_ref, kc_ref, kp_ref, vc_ref, vp_ref, bias_ref, o_ref, lse_ref, *, dilation):
    blk = ATT_BLOCK
    n = pl.program_id(1)
    lane = lax.broadcasted_iota(jnp.int32, (blk, LANES), 1)
    in_head_a = lane < HEAD_DIM
    scale_a = jnp.where(in_head_a, HEAD_DIM ** -0.5, 0.0).astype(BF16)
    scale_b = jnp.where(in_head_a, 0.0, HEAD_DIM ** -0.5).astype(BF16)
    ones = jnp.ones((2 * blk, LANES), BF16)

    for g in range(ATT_SUB):
        if dilation == 1 and g > 0:
            kprev_ref, vprev_ref, gp, has_prev = kc_ref, vc_ref, g - 1, None
        elif dilation == 1:
            kprev_ref, vprev_ref, gp, has_prev = kp_ref, vp_ref, 0, n > 0
        else:
            kprev_ref, vprev_ref, gp, has_prev = kp_ref, vp_ref, g, n > 0
        stats = jnp.zeros((blk, LANES), F32)
        for tile in range(N_HEAD_TILES):
            cols = slice(tile * LANES, (tile + 1) * LANES)
            q = q_ref[0, g, :, cols]
            q2 = jnp.concatenate([q * scale_a, q * scale_b], axis=0)
            k2 = jnp.concatenate([kprev_ref[0, gp, :, cols], kc_ref[0, g, :, cols]], axis=0)
            s = lax.dot_general(q2, k2, (((1,), (1,)), ((), ())), preferred_element_type=F32)
            if has_prev is None:
                s = s + bias_ref[1, tile]
            else:
                s = s + bias_ref[has_prev.astype(jnp.int32), tile]
            m = jnp.max(s, axis=-1, keepdims=True)
            p = jnp.exp(s - m).astype(BF16)
            v2 = jnp.concatenate([vprev_ref[0, gp, :, cols], vc_ref[0, g, :, cols]], axis=0)
            r = jnp.dot(p, jnp.concatenate([v2, ones], axis=1), preferred_element_type=F32)
            acc, den = r[:, :LANES], r[:, LANES:]
            o = acc / den
            o_ref[0, g, :, cols] = jnp.where(in_head_a, o[:blk], o[blk:]).astype(BF16)
            lse = m + jnp.log(den)
            stats = jnp.where(lane == tile * HEADS_PER_TILE, lse[:blk], stats)
            stats = jnp.where(lane == tile * HEADS_PER_TILE + 1, lse[blk:], stats)
        lse_ref[0, g] = stats


def _attention(q, k, v, dilation, batch):
    spans, residues, blk, _ = q.shape
    per_batch = spans // batch
    d = dilation
    if d == 16:
        grid = (batch, per_batch, residues // ATT_SUB)
        cur_map = lambda bi, n, rg: (bi * per_batch + n, rg, 0, 0)
        prev_map = lambda bi, n, rg: (bi * per_batch + jnp.maximum(n - 1, 0), rg, 0, 0)
        prev_sub = ATT_SUB
    else:
        grid = (batch, per_batch)
        cur_map = lambda bi, n: (bi * per_batch + n, 0, 0, 0)
        if d == 1:
            prev_map = lambda bi, n: (bi * per_batch + jnp.maximum(n - 1, 0), ATT_SUB - 1, 0, 0)
            prev_sub = 1
        else:
            prev_map = lambda bi, n: (bi * per_batch + jnp.maximum(n - 1, 0), 0, 0, 0)
            prev_sub = ATT_SUB
    cur = lambda w: pl.BlockSpec((1, ATT_SUB, blk, w), cur_map)
    prev = lambda w: pl.BlockSpec((1, prev_sub, blk, w), prev_map)
    bias = _attention_bias(d)
    return pl.pallas_call(
        functools.partial(_attention_kernel, dilation=d),
        grid=grid,
        in_specs=[cur(ATT_WIDTH), cur(ATT_WIDTH), prev(ATT_WIDTH), cur(ATT_WIDTH), prev(ATT_WIDTH),
                  _const_spec(bias.shape)],
        out_specs=[cur(ATT_WIDTH), cur(LANES)],
        out_shape=[jax.ShapeDtypeStruct(q.shape, BF16),
                   jax.ShapeDtypeStruct((spans, residues, blk, LANES), F32)],
        compiler_params=_params(len(grid)),
        name=f"attention_d{d}",
    )(q, k, k, v, v, bias)


def _split3(x):
    hi = x.astype(BF16)
    r = x - hi.astype(F32)
    mid = r.astype(BF16)
    lo = (r - mid.astype(F32)).astype(BF16)
    return hi, mid, lo


_LOG2E = 1.4426950408889634


def _ssd_kernel(xc_ref, zs_ref, dt_ref, alog_ref, dskip_ref, nw_ref, y_ref, st_ref):
    ch = SSM_CHUNK

    @pl.when(pl.program_id(1) == 0)
    def _():
        st_ref[...] = jnp.zeros_like(st_ref)

    dt = dt_ref[0]
    la = dt * (-_LOG2E * jnp.exp(alog_ref[...]))
    row_i = lax.broadcasted_iota(jnp.int32, (ch, ch), 0)
    col_i = lax.broadcasted_iota(jnp.int32, (ch, ch), 1)
    causal = row_i >= col_i
    tri = jnp.where(causal, 1.0, 0.0).astype(BF16)
    a_cs = sum(jnp.dot(tri, part, preferred_element_type=F32) for part in _split3(la))
    a_cs_t = a_cs.T
    dt_t = dt.T
    a_src_t = a_cs_t - jnp.log2(dt_t)
    w_t = jnp.exp2(a_cs_t[:, ch - 1:ch] - a_cs_t) * dt_t
    lane2 = lax.broadcasted_iota(jnp.int32, (ch, GROUP_WIDTH), 1)
    lane1 = lax.broadcasted_iota(jnp.int32, (ch, LANES), 1)

    for g in range(SSM_GROUPS):
        gs = slice(g * GROUP_WIDTH, (g + 1) * GROUP_WIDTH)
        b_bf = xc_ref[0, :, SSM_INNER + g * SSM_STATE:SSM_INNER + (g + 1) * SSM_STATE]
        c_bf = xc_ref[0, :, SSM_INNER + (SSM_GROUPS + g) * SSM_STATE:
                      SSM_INNER + (SSM_GROUPS + g + 1) * SSM_STATE]
        cb = lax.dot_general(c_bf, b_bf, (((1,), (1,)), ((), ())), preferred_element_type=F32)
        b_t = b_bf.astype(F32).T
        x_bf = xc_ref[0, :, gs]

        intra, to_state, a_cols, x_diag = [], [], [], []
        for hh in range(HEADS_PER_GROUP):
            h = g * HEADS_PER_GROUP + hh
            a_col = jnp.broadcast_to(a_cs[:, h:h + 1], (ch, ch))
            decay_dt = jnp.exp2(jnp.where(causal, a_col - a_src_t[h:h + 1, :], NEG))
            intra.append((cb * decay_dt).astype(BF16))
            to_state.append((b_t * w_t[h:h + 1, :]).astype(BF16))
            a_cols.append(a_col)
            in_head = (lane2 >= hh * HEAD_DIM) & (lane2 < (hh + 1) * HEAD_DIM)
            x_diag.append(jnp.where(in_head, x_bf, jnp.zeros_like(x_bf)))
        lhs = jnp.concatenate([jnp.concatenate(intra, axis=1), jnp.concatenate(to_state, axis=1)], axis=0)
        r = jnp.dot(lhs, jnp.concatenate(x_diag, axis=0), preferred_element_type=F32)
        y_diag, st_inc = r[:ch], r[ch:]

        first = lane1 < HEAD_DIM
        a_exp = jnp.exp2(jnp.concatenate([jnp.where(first, a_cols[0], a_cols[1]),
                                          jnp.where(first, a_cols[2], a_cols[3])], axis=1))
        st_old = st_ref[g]
        y_off = jnp.dot(c_bf, st_old.astype(BF16), preferred_element_type=F32) * a_exp
        st_ref[g] = st_old * a_exp[ch - 1:ch, :] + st_inc

        y = y_diag + y_off + dskip_ref[:, gs] * x_bf.astype(F32)
        y = y * zs_ref[0, :, gs].astype(F32)
        y_ref[0, :, gs] = (y * _rms_scale(y) * nw_ref[:, gs]).astype(BF16)


def _ssd(xc, zs, dt, a_log_pad, d_skip_wide, norm_w):
    b, s, _ = xc.shape
    ch = SSM_CHUNK
    blk = lambda w: pl.BlockSpec((1, ch, w), lambda bi, c: (bi, c, 0))
    return pl.pallas_call(
        _ssd_kernel,
        grid=(b, s // ch),
        in_specs=[blk(CONV_DIM), blk(SSM_INNER), blk(LANES),
                  _const_spec((1, LANES)), _const_spec((1, SSM_INNER)), _const_spec((1, SSM_INNER))],
        out_specs=blk(SSM_INNER),
        out_shape=jax.ShapeDtypeStruct((b, s, SSM_INNER), BF16),
        scratch_shapes=[pltpu.VMEM((SSM_GROUPS, SSM_STATE, GROUP_WIDTH), F32)],
        compiler_params=_params(2),
        name="ssd",
    )(xc, zs, dt, a_log_pad, d_skip_wide, norm_w)


_ATT_SLABS = ATT_WIDTH // LANES


def _merge_kernel(x_ref, o1_ref, o4_ref, o16_ref, l1_ref, l4_ref, l16_ref, y_ref, gate_ref,
                  expand_ref, watt_ref, wssm_ref, wout_ref, nw_ref, h_ref, onat_ref, lnat_ref):
    tm = ROW_TILE
    for r in range(4):
        lnat_ref[0, pl.ds(r, tm // 4, stride=4), :] = l4_ref[0, r]
    for r in range(16):
        lnat_ref[1, pl.ds(r, tm // 16, stride=16), :] = l16_ref[0, r]
    for slab in range(_ATT_SLABS):
        cols = slice(slab * LANES, (slab + 1) * LANES)
        for r in range(4):
            onat_ref[0, slab, pl.ds(r, tm // 4, stride=4), :] = o4_ref[0, r, :, cols].astype(F32)
        for r in range(16):
            onat_ref[1, slab, pl.ds(r, tm // 16, stride=16), :] = o16_ref[0, r, :, cols].astype(F32)

    lses = [l1_ref[...], lnat_ref[0], lnat_ref[1]]
    top = jnp.maximum(jnp.maximum(lses[0], lses[1]), lses[2])
    wts = [jnp.exp(l - top) for l in lses]
    inv = 1.0 / (wts[0] + wts[1] + wts[2])
    wts = [(w * inv).astype(BF16) for w in wts]
    slabs = []
    for slab in range(_ATT_SLABS):
        cols = slice(slab * LANES, (slab + 1) * LANES)
        wide = [jnp.dot(w, expand_ref[:, cols], preferred_element_type=F32) for w in wts]
        slabs.append(wide[0] * o1_ref[:, cols].astype(F32) + wide[1] * onat_ref[0, slab]
                     + wide[2] * onat_ref[1, slab])
    att = jnp.concatenate(slabs, axis=1).astype(BF16)
    att = jnp.dot(att, watt_ref[...], preferred_element_type=F32)
    ssm = jnp.dot(y_ref[...], wssm_ref[...], preferred_element_type=F32)
    gates = gate_ref[...].astype(F32)
    mixed = gates[:, :D_MODEL] * att + gates[:, D_MODEL:] * ssm
    mixed = jnp.dot(mixed.astype(BF16), wout_ref[...], preferred_element_type=F32)
    h_ref[...] = x_ref[...] + mixed * _rms_scale(mixed) * nw_ref[...]


def _merge(x2, outs, lses, y, gates, expand, w_att, w_ssm, w_out, norm_w):
    t = x2.shape[0]
    tm = ROW_TILE
    row = lambda width: pl.BlockSpec((tm, width), lambda i: (i, 0))
    return pl.pallas_call(
        _merge_kernel,
        grid=(t // tm,),
        in_specs=[row(D_MODEL), row(ATT_WIDTH), _residue4_spec(ATT_WIDTH), _residue16_spec(ATT_WIDTH),
                  row(LANES), _residue4_spec(LANES), _residue16_spec(LANES),
                  row(SSM_INNER), row(2 * D_MODEL),
                  _const_spec((LANES, ATT_WIDTH)), _const_spec((ATT_WIDTH, D_MODEL)),
                  _const_spec((SSM_INNER, D_MODEL)), _const_spec((D_MODEL, D_MODEL)),
                  _const_spec((1, D_MODEL))],
        out_specs=row(D_MODEL),
        out_shape=jax.ShapeDtypeStruct((t, D_MODEL), F32),
        scratch_shapes=[pltpu.VMEM((2, _ATT_SLABS, tm, LANES), F32),
                        pltpu.VMEM((2, tm, LANES), F32)],
        compiler_params=_params(1),
        name="merge",
    )(x2, *outs, *lses, y, gates, expand, w_att, w_ssm, w_out, norm_w)


FFN_CHUNK = 1024


def _ffn_kernel(h_ref, pre_ref, wup_ref, wdown_ref, post_ref, o_ref):
    h = h_ref[...]
    f = (h * _rms_scale(h) * pre_ref[...]).astype(BF16)
    acc = jnp.zeros((ROW_TILE, D_MODEL), F32)
    for c in range(0, FFN_HIDDEN, FFN_CHUNK):
        up = jnp.dot(f, wup_ref[:, c:c + FFN_CHUNK], preferred_element_type=F32)
        act = jnp.square(jnp.maximum(up, 0.0)).astype(BF16)
        acc = acc + jnp.dot(act, wdown_ref[c:c + FFN_CHUNK, :], preferred_element_type=F32)
    o_ref[...] = h + acc * _rms_scale(acc) * post_ref[...]


def _ffn(h2, pre_w, w_up, w_down, post_w):
    t = h2.shape[0]
    row = pl.BlockSpec((ROW_TILE, D_MODEL), lambda i: (i, 0))
    return pl.pallas_call(
        _ffn_kernel,
        grid=(t // ROW_TILE,),
        in_specs=[row, _const_spec((1, D_MODEL)), _const_spec((D_MODEL, FFN_HIDDEN)),
                  _const_spec((FFN_HIDDEN, D_MODEL)), _const_spec((1, D_MODEL))],
        out_specs=row,
        out_shape=jax.ShapeDtypeStruct((t, D_MODEL), F32),
        compiler_params=_params(1),
        name="ffn",
    )(h2, pre_w, w_up, w_down, post_w)


def _pad_lanes(v):
    return jnp.pad(v.astype(F32), (0, LANES - v.shape[0])).reshape(1, LANES)


def _head_expand_matrix():
    e = np.zeros((LANES, ATT_WIDTH), np.float32)
    for h in range(N_ATT_HEADS):
        e[h, h * HEAD_DIM:(h + 1) * HEAD_DIM] = 1.0
    return jnp.asarray(e, dtype=BF16)


def kernel(x, norm_mix_pre_w, w_in, b_gate, conv_w, conv_b, dt_bias, a_log, d_skip, ssm_norm_w,
           w_att_proj, w_ssm_proj, w_out, norm_mix_post_w, norm_ffn_pre_w, w_up, w_down,
           norm_ffn_post_w):
    b, s, d_model = x.shape
    depth = w_in.shape[0]
    assert d_model == D_MODEL and s % SPAN16 == 0
    t = b * s
    expand = _head_expand_matrix()
    h = x.reshape(t, D_MODEL)
    for layer in range(depth):
        w = w_in[layer]
        off_ssm = 3 * ATT_WIDTH
        off_dt = off_ssm + SSM_INNER + CONV_DIM
        w_qkv = w[:, :off_ssm].astype(BF16)
        w_ssm = jnp.concatenate(
            [0.5 * w[:, off_ssm:off_ssm + SSM_INNER], w[:, off_ssm + SSM_INNER:off_dt],
             w[:, off_dt + SSM_HEADS:],
             jnp.pad(w[:, off_dt:off_dt + SSM_HEADS], ((0, 0), (0, LANES - SSM_HEADS)))],
            axis=1).astype(BF16)
        pre_w = norm_mix_pre_w[layer].reshape(1, -1)
        qkv = _qkv_proj(h, pre_w, w_qkv)
        zs, xc, gates, dt = _ssm_proj(h, pre_w, w_ssm, b_gate[layer].reshape(1, -1),
                                      _pad_lanes(dt_bias[layer]), 0.5 * conv_w[layer],
                                      0.5 * conv_b[layer].reshape(1, -1), s)
        outs, lses = [], []
        for idx, d in enumerate(DILATIONS):
            q, k, v = qkv[idx], qkv[3 + idx], qkv[6 + idx]
            if d == 1:
                q, k, v = [a.reshape(t // SPAN4, 4, ATT_BLOCK, ATT_WIDTH) for a in (q, k, v)]
            o, lse = _attention(q, k, v, d, b)
            if d == 1:
                o, lse = o.reshape(t, ATT_WIDTH), lse.reshape(t, LANES)
            outs.append(o)
            lses.append(lse)
        seq = lambda a: a.reshape(b, s, a.shape[-1])
        y = _ssd(seq(xc), seq(zs), seq(dt), _pad_lanes(a_log[layer]),
                 jnp.repeat(d_skip[layer].astype(F32), HEAD_DIM).reshape(1, -1),
                 ssm_norm_w[layer].reshape(1, -1))
        h = _merge(h, outs, lses, y.reshape(t, SSM_INNER), gates, expand,
                   w_att_proj[layer].astype(BF16), w_ssm_proj[layer].astype(BF16),
                   w_out[layer].astype(BF16), norm_mix_post_w[layer].reshape(1, -1))
        h = _ffn(h, norm_ffn_pre_w[layer].reshape(1, -1), w_up[layer].astype(BF16),
                 w_down[layer].astype(BF16), norm_ffn_post_w[layer].reshape(1, -1))
    return h.reshape(b, s, D_MODEL)
```

```python
import functools
import math

import numpy as np
import jax
import jax.numpy as jnp
from jax import lax
from jax.experimental import pallas as pl
from jax.experimental.pallas import tpu as pltpu

F32 = jnp.float32
BF16 = jnp.bfloat16

D_MODEL = 1024
HEAD_DIM = 64
N_ATT_HEADS = 12
ATT_WIDTH = N_ATT_HEADS * HEAD_DIM
DILATIONS = (1, 4, 16)
ATT_BLOCK = 128
SSM_INNER = 2048
SSM_HEADS = 32
SSM_GROUPS = 8
SSM_STATE = 128
SSM_CONV = 4
SSM_CHUNK = 128
CONV_DIM = SSM_INNER + 2 * SSM_GROUPS * SSM_STATE
FFN_HIDDEN = 4 * D_MODEL
RMS_EPS = 1e-6
NEG = -1e30

LANES = 128
MXU_N = 256
HEADS_PER_TILE = LANES // HEAD_DIM
N_HEAD_TILES = N_ATT_HEADS // HEADS_PER_TILE
GROUP_WIDTH = SSM_INNER // SSM_GROUPS
HEADS_PER_GROUP = SSM_HEADS // SSM_GROUPS
VMEM_LIMIT_BYTES = 56 * 1024 * 1024

ROW_TILE = 512
SPAN4 = ATT_BLOCK * 4
SPAN16 = ATT_BLOCK * 16
assert ROW_TILE == SPAN4


def _const_spec(shape):
    return pl.BlockSpec(shape, lambda *_: (0,) * len(shape), pipeline_mode=pl.Buffered(1))


def _params(n_axes):
    return pltpu.CompilerParams(dimension_semantics=("arbitrary",) * n_axes,
                                vmem_limit_bytes=VMEM_LIMIT_BYTES)


def _rms_scale(x):
    return lax.rsqrt(jnp.mean(x * x, axis=-1, keepdims=True) + RMS_EPS)


def _residue16_spec(width):
    per_span = SPAN16 // ROW_TILE
    return pl.BlockSpec((1, 16, ROW_TILE // 16, width), lambda i: (i // per_span, 0, i % per_span, 0))


def _residue4_spec(width):
    return pl.BlockSpec((1, 4, ATT_BLOCK, width), lambda i: (i, 0, 0, 0))


_OFF_Z = 0
_OFF_XBC = _OFF_Z + SSM_INNER
_OFF_GATE = _OFF_XBC + CONV_DIM
_OFF_DT = _OFF_GATE + 2 * D_MODEL
_W_SSM = _OFF_DT + LANES
_QKV_SLABS = 3 * ATT_WIDTH // LANES
_LOG2E = 1.4426950408889634
Q_SCALE = HEAD_DIM ** -0.5 * _LOG2E


def _qkv_kernel(x_ref, nw_ref, w_ref, *refs):
    out_refs, (u_ref, slab_ref, res4_ref) = refs[:9], refs[9:]
    tm = ROW_TILE
    x = x_ref[...]
    u_ref[...] = (x * _rms_scale(x) * nw_ref[...]).astype(BF16)
    for a in range(3):
        nat_ref, d4_ref, d16_ref = out_refs[3 * a:3 * a + 3]
        for c in range(0, ATT_WIDTH, MXU_N):
            acc = jnp.dot(u_ref[...], w_ref[:, a * ATT_WIDTH + c:a * ATT_WIDTH + c + MXU_N],
                          preferred_element_type=F32)
            if a == 0:
                acc = acc * Q_SCALE
            nat_ref[:, c:c + MXU_N] = acc.astype(BF16)
            for half in range(MXU_N // LANES):
                cols = slice(c + half * LANES, c + (half + 1) * LANES)
                slab = (a * ATT_WIDTH + c) // LANES + half
                slab_ref[slab] = acc[:, half * LANES:(half + 1) * LANES]
                for r in range(4):
                    part = slab_ref[slab, pl.ds(r, tm // 4, stride=4), :]
                    d4_ref[0, r, :, cols] = part.astype(BF16)
                    res4_ref[slab, r] = part
                for r in range(16):
                    part = res4_ref[slab, r % 4, pl.ds(r // 4, tm // 16, stride=4), :]
                    d16_ref[0, r, :, cols] = part.astype(BF16)


def _qkv_proj(x2, norm_w, w_qkv):
    t = x2.shape[0]
    tm = ROW_TILE
    nat = pl.BlockSpec((tm, ATT_WIDTH), lambda i: (i, 0))
    shapes = [jax.ShapeDtypeStruct((t, ATT_WIDTH), BF16),
              jax.ShapeDtypeStruct((t // SPAN4, 4, ATT_BLOCK, ATT_WIDTH), BF16),
              jax.ShapeDtypeStruct((t // SPAN16, 16, ATT_BLOCK, ATT_WIDTH), BF16)]
    return pl.pallas_call(
        _qkv_kernel,
        grid=(t // tm,),
        in_specs=[pl.BlockSpec((tm, D_MODEL), lambda i: (i, 0)), _const_spec((1, D_MODEL)),
                  _const_spec((D_MODEL, 3 * ATT_WIDTH))],
        out_specs=[nat, _residue4_spec(ATT_WIDTH), _residue16_spec(ATT_WIDTH)] * 3
                  + [pl.BlockSpec((tm, D_MODEL), lambda i: (i, 0))],
        out_shape=shapes * 3 + [jax.ShapeDtypeStruct((t, D_MODEL), BF16)],
        scratch_shapes=[pltpu.VMEM((_QKV_SLABS, tm, LANES), F32),
                        pltpu.VMEM((_QKV_SLABS, 4, tm // 4, LANES), F32)],
        compiler_params=_params(1),
        name="qkv_proj",
    )(x2, norm_w, w_qkv)


_HALO = 8
_PAD_BUFFERS = 8


def _split3(x):
    hi = x.astype(BF16)
    r = x - hi.astype(F32)
    mid = r.astype(BF16)
    lo = (r - mid.astype(F32)).astype(BF16)
    return hi, mid, lo


def _chunk_tri_matrix():
    i = np.arange(ROW_TILE)
    same_chunk = (i[:, None] // SSM_CHUNK) == (i[None, :] // SSM_CHUNK)
    return jnp.asarray(same_chunk & (i[:, None] >= i[None, :]), dtype=BF16)


def _ssm_proj_kernel(u_ref, w_ref, bg_ref, dtb_ref, cw_ref, cb_ref, alog_ref, tri_ref,
                     zs_ref, xc_ref, gate_ref, acs_ref, asrc_ref, wt_ref, halo_ref, pad_ref,
                     *, tiles_per_seq):
    tm = ROW_TILE

    @pl.when(pl.program_id(0) % tiles_per_seq == 0)
    def _():
        halo_ref[...] = jnp.zeros_like(halo_ref)

    def proj(off, width):
        return jnp.dot(u_ref[...], w_ref[:, off:off + width], preferred_element_type=F32)

    dt_raw = proj(_OFF_DT, LANES) + dtb_ref[...]
    dt = jnp.maximum(dt_raw, 0.0) + jnp.log1p(jnp.exp(-jnp.abs(dt_raw)))
    la = dt * (-_LOG2E * jnp.exp(alog_ref[...]))
    a_cs = sum(jnp.dot(tri_ref[...], part, preferred_element_type=F32) for part in _split3(la))
    acs_ref[...] = a_cs
    ch = SSM_CHUNK
    for cidx in range(tm // ch):
        a_t = a_cs[cidx * ch:(cidx + 1) * ch].T
        dt_t = dt[cidx * ch:(cidx + 1) * ch].T
        asrc_ref[cidx] = (a_t - jnp.log2(dt_t))[:SSM_HEADS]
        wt_ref[cidx] = (jnp.exp2(a_t[:, ch - 1:ch] - a_t) * dt_t)[:SSM_HEADS]

    def z_chunk(c):
        h = proj(_OFF_Z + c, MXU_N)
        zs_ref[:, c:c + MXU_N] = (h + h * jnp.tanh(h)).astype(BF16)

    def gate_chunk(c):
        logits = proj(_OFF_GATE + c, MXU_N) + bg_ref[:, c:c + MXU_N]
        gate_ref[:, c:c + MXU_N] = jax.nn.sigmoid(logits).astype(BF16)

    def conv_chunk(idx):
        acc2 = proj(_OFF_XBC + idx * MXU_N, MXU_N)
        for half in range(MXU_N // LANES):
            c = idx * MXU_N + half * LANES
            cs = slice(c, c + LANES)
            buf = (c // LANES) % _PAD_BUFFERS
            acc = acc2[:, half * LANES:(half + 1) * LANES]
            pad_ref[buf, pl.ds(0, _HALO, stride=2), :] = halo_ref[:, cs]
            pad_ref[buf, pl.ds(2 * _HALO, tm, stride=2), :] = acc
            halo_ref[:, cs] = acc[tm - _HALO:, :]
            h = cb_ref[:, cs] + cw_ref[SSM_CONV - 1:SSM_CONV, cs] * acc
            for tap in range(SSM_CONV - 1):
                start = 2 * (_HALO - (SSM_CONV - 1) + tap)
                h = h + cw_ref[tap:tap + 1, cs] * pad_ref[buf, pl.ds(start, tm, stride=2), :]
            xc_ref[:, cs] = (h + h * jnp.tanh(h)).astype(BF16)

    light = ([functools.partial(z_chunk, c) for c in range(0, SSM_INNER, MXU_N)]
             + [functools.partial(gate_chunk, c) for c in range(0, 2 * D_MODEL, MXU_N)])
    for idx in range(CONV_DIM // MXU_N):
        conv_chunk(idx)
        light[idx]()


def _ssm_proj(u, w_ssm, b_gate, dt_bias_pad, conv_w_half, conv_b_half, a_log_pad, seq_len):
    t = u.shape[0]
    tm = ROW_TILE
    row = lambda width: pl.BlockSpec((tm, width), lambda i: (i, 0))
    per_chunk = pl.BlockSpec((tm // SSM_CHUNK, SSM_HEADS, SSM_CHUNK), lambda i: (i, 0, 0))
    widths = (SSM_INNER, CONV_DIM, 2 * D_MODEL)
    out_shape = [jax.ShapeDtypeStruct((t, w), BF16) for w in widths]
    out_shape.append(jax.ShapeDtypeStruct((t, LANES), F32))
    out_shape += [jax.ShapeDtypeStruct((t // SSM_CHUNK, SSM_HEADS, SSM_CHUNK), F32)] * 2
    tri = _chunk_tri_matrix()
    return pl.pallas_call(
        functools.partial(_ssm_proj_kernel, tiles_per_seq=seq_len // tm),
        grid=(t // tm,),
        in_specs=[row(D_MODEL), _const_spec((D_MODEL, _W_SSM)),
                  _const_spec((1, 2 * D_MODEL)), _const_spec((1, LANES)),
                  _const_spec((SSM_CONV, CONV_DIM)), _const_spec((1, CONV_DIM)),
                  _const_spec((1, LANES)), _const_spec(tri.shape)],
        out_specs=[row(w) for w in widths] + [row(LANES), per_chunk, per_chunk],
        out_shape=out_shape,
        scratch_shapes=[pltpu.VMEM((_HALO, CONV_DIM), F32),
                        pltpu.VMEM((_PAD_BUFFERS, 2 * (_HALO + tm), LANES), F32)],
        compiler_params=_params(1),
        name="ssm_proj",
    )(u, w_ssm, b_gate, dt_bias_pad, conv_w_half, conv_b_half, a_log_pad, tri)


ATT_SUB = 4


def _alibi_slopes(n):
    def pow2(m):
        start = 2.0 ** (-8.0 / m)
        return [start ** (i + 1) for i in range(m)]
    if (n & (n - 1)) == 0:
        s = pow2(n)
    else:
        c = 2 ** int(math.floor(math.log2(n)))
        s = pow2(c) + pow2(2 * c)[0::2][: n - c]
    return np.array(s, dtype=np.float32)


def _attention_bias(dilation):
    blk = ATT_BLOCK
    i = np.arange(blk)[:, None]
    j = np.arange(blk)[None, :]
    dist_prev = blk + i - j
    dist_cur = i - j
    slopes = (_alibi_slopes(N_ATT_HEADS) * np.float32(_LOG2E)).reshape(N_HEAD_TILES, HEADS_PER_TILE)
    out = np.empty((2, N_HEAD_TILES, HEADS_PER_TILE * blk, 2 * blk), np.float32)
    for tile in range(N_HEAD_TILES):
        for hh in range(HEADS_PER_TILE):
            s = slopes[tile, hh]
            prev = np.where(dist_prev <= blk, -s * (dist_prev * dilation).astype(np.float32), NEG)
            cur = np.where(dist_cur >= 0, -s * (dist_cur * dilation).astype(np.float32), NEG)
            rows = slice(hh * blk, (hh + 1) * blk)
            out[1, tile, rows, :blk] = prev
            out[0, tile, rows, :blk] = NEG
            out[:, tile, rows, blk:] = cur
    return jnp.asarray(out)


DEN_LANE = 16
assert N_ATT_HEADS <= DEN_LANE


def _attention_kernel(q_ref, kc_ref, kp_ref, vc_ref, vp_ref, bias_ref, o_ref, st_ref, *, dilation):
    blk = ATT_BLOCK
    n = pl.program_id(1)
    lane = lax.broadcasted_iota(jnp.int32, (blk, LANES), 1)
    in_head_a = lane < HEAD_DIM
    keep_a = jnp.where(in_head_a, 1.0, 0.0).astype(BF16)
    keep_b = jnp.where(in_head_a, 0.0, 1.0).astype(BF16)
    ones = jnp.ones((2 * blk, LANES), BF16)

    for g in range(ATT_SUB):
        if dilation == 1 and g > 0:
            kprev_ref, vprev_ref, gp, has_prev = kc_ref, vc_ref, g - 1, None
        elif dilation == 1:
            kprev_ref, vprev_ref, gp, has_prev = kp_ref, vp_ref, 0, n > 0
        else:
            kprev_ref, vprev_ref, gp, has_prev = kp_ref, vp_ref, g, n > 0
        stats = jnp.zeros((blk, LANES), F32)
        for tile in range(N_HEAD_TILES):
            cols = slice(tile * LANES, (tile + 1) * LANES)
            q = q_ref[0, g, :, cols]
            q2 = jnp.concatenate([q * keep_a, q * keep_b], axis=0)
            k2 = jnp.concatenate([kprev_ref[0, gp, :, cols], kc_ref[0, g, :, cols]], axis=0)
            s = lax.dot_general(q2, k2, (((1,), (1,)), ((), ())), preferred_element_type=F32)
            if has_prev is None:
                s = s + bias_ref[1, tile]
            else:
                s = s + bias_ref[has_prev.astype(jnp.int32), tile]
            m = jnp.max(s, axis=-1, keepdims=True)
            p = jnp.exp2(s - m).astype(BF16)
            v2 = jnp.concatenate([vprev_ref[0, gp, :, cols], vc_ref[0, g, :, cols]], axis=0)
            r = jnp.dot(p, jnp.concatenate([v2, ones], axis=1), preferred_element_type=F32)
            acc, den = r[:, :LANES], r[:, LANES:]
            o_ref[0, g, :, cols] = jnp.where(in_head_a, acc[:blk], acc[blk:]).astype(BF16)
            for hh in range(HEADS_PER_TILE):
                head = tile * HEADS_PER_TILE + hh
                stats = jnp.where(lane == head, m[hh * blk:(hh + 1) * blk], stats)
                stats = jnp.where(lane == DEN_LANE + head, den[hh * blk:(hh + 1) * blk], stats)
        st_ref[0, g] = stats


def _attention(q, k, v, dilation, batch):
    spans, residues, blk, _ = q.shape
    per_batch = spans // batch
    d = dilation
    if d == 16:
        grid = (batch, per_batch, residues // ATT_SUB)
        cur_map = lambda bi, n, rg: (bi * per_batch + n, rg, 0, 0)
        prev_map = lambda bi, n, rg: (bi * per_batch + jnp.maximum(n - 1, 0), rg, 0, 0)
        prev_sub = ATT_SUB
    else:
        grid = (batch, per_batch)
        cur_map = lambda bi, n: (bi * per_batch + n, 0, 0, 0)
        if d == 1:
            prev_map = lambda bi, n: (bi * per_batch + jnp.maximum(n - 1, 0), ATT_SUB - 1, 0, 0)
            prev_sub = 1
        else:
            prev_map = lambda bi, n: (bi * per_batch + jnp.maximum(n - 1, 0), 0, 0, 0)
            prev_sub = ATT_SUB
    cur = lambda w: pl.BlockSpec((1, ATT_SUB, blk, w), cur_map)
    prev = lambda w: pl.BlockSpec((1, prev_sub, blk, w), prev_map)
    bias = _attention_bias(d)
    return pl.pallas_call(
        functools.partial(_attention_kernel, dilation=d),
        grid=grid,
        in_specs=[cur(ATT_WIDTH), cur(ATT_WIDTH), prev(ATT_WIDTH), cur(ATT_WIDTH), prev(ATT_WIDTH),
                  _const_spec(bias.shape)],
        out_specs=[cur(ATT_WIDTH), cur(LANES)],
        out_shape=[jax.ShapeDtypeStruct(q.shape, BF16),
                   jax.ShapeDtypeStruct((spans, residues, blk, LANES), F32)],
        compiler_params=_params(len(grid)),
        name=f"attention_d{d}",
    )(q, k, k, v, v, bias)


SSD_SUB = 2


def _head_lane_matrix(n_heads, rows):
    e = np.zeros((rows, n_heads * HEAD_DIM), np.float32)
    for h in range(n_heads):
        e[h, h * HEAD_DIM:(h + 1) * HEAD_DIM] = 1.0
    return jnp.asarray(e, dtype=BF16)


def _group_head_masks():
    m = np.zeros((HEADS_PER_GROUP, SSM_CHUNK, GROUP_WIDTH), np.float32)
    for hh in range(HEADS_PER_GROUP):
        m[hh, :, hh * HEAD_DIM:(hh + 1) * HEAD_DIM] = 1.0
    return jnp.asarray(m, dtype=BF16)


def _ssd_kernel(xc_ref, zs_ref, acs_ref, asrc_ref, wt_ref, spread_ref, mask_ref, dskip_ref, nw_ref,
                y_ref, st_ref):
    @pl.when(pl.program_id(1) == 0)
    def _():
        st_ref[...] = jnp.zeros_like(st_ref)

    for sub in range(SSD_SUB):
        _ssd_chunk(sub, xc_ref, zs_ref, acs_ref, asrc_ref, wt_ref, spread_ref, mask_ref, dskip_ref,
                   nw_ref, y_ref, st_ref)


def _ssd_chunk(sub, xc_ref, zs_ref, acs_ref, asrc_ref, wt_ref, spread_ref, mask_ref, dskip_ref,
               nw_ref, y_ref, st_ref):
    ch = SSM_CHUNK
    rows = slice(sub * ch, (sub + 1) * ch)
    a_cs = acs_ref[0, rows, :]
    a_src_t = asrc_ref[0, sub]
    w_t = wt_ref[0, sub]
    row_i = lax.broadcasted_iota(jnp.int32, (ch, ch), 0)
    col_i = lax.broadcasted_iota(jnp.int32, (ch, ch), 1)
    causal = row_i >= col_i
    e = jnp.exp2(a_cs)
    e_hi = e.astype(BF16)
    e_lo = (e - e_hi.astype(F32)).astype(BF16)
    a_exp_all = (jnp.dot(e_hi, spread_ref[...], preferred_element_type=F32)
                 + jnp.dot(e_lo, spread_ref[...], preferred_element_type=F32))

    for g in range(SSM_GROUPS):
        gs = slice(g * GROUP_WIDTH, (g + 1) * GROUP_WIDTH)
        b_bf = xc_ref[0, rows, SSM_INNER + g * SSM_STATE:SSM_INNER + (g + 1) * SSM_STATE]
        c_bf = xc_ref[0, rows, SSM_INNER + (SSM_GROUPS + g) * SSM_STATE:
                      SSM_INNER + (SSM_GROUPS + g + 1) * SSM_STATE]
        cb = lax.dot_general(c_bf, b_bf, (((1,), (1,)), ((), ())), preferred_element_type=F32)
        b_t = b_bf.astype(F32).T
        x_bf = xc_ref[0, rows, gs]

        intra, to_state, x_diag = [], [], []
        for hh in range(HEADS_PER_GROUP):
            h = g * HEADS_PER_GROUP + hh
            a_col = jnp.broadcast_to(a_cs[:, h:h + 1], (ch, ch))
            decay_dt = jnp.exp2(jnp.where(causal, a_col - a_src_t[h:h + 1, :], NEG))
            intra.append((cb * decay_dt).astype(BF16))
            to_state.append((b_t * w_t[h:h + 1, :]).astype(BF16))
            x_diag.append(x_bf * mask_ref[hh])
        lhs = jnp.concatenate([jnp.concatenate(intra, axis=1), jnp.concatenate(to_state, axis=1)], axis=0)
        r = jnp.dot(lhs, jnp.concatenate(x_diag, axis=0), preferred_element_type=F32)
        y_diag, st_inc = r[:ch], r[ch:]

        a_exp = a_exp_all[:, gs]
        st_old = st_ref[g]
        y_off = jnp.dot(c_bf, st_old.astype(BF16), preferred_element_type=F32) * a_exp
        st_ref[g] = st_old * a_exp[ch - 1:ch, :] + st_inc

        y = y_diag + y_off + dskip_ref[:, gs] * x_bf.astype(F32)
        y = y * zs_ref[0, rows, gs].astype(F32)
        y_ref[0, rows, gs] = (y * _rms_scale(y) * nw_ref[:, gs]).astype(BF16)


def _ssd(xc, zs, a_cs, a_src_t, w_t, d_skip_wide, norm_w):
    b, s, _ = xc.shape
    ch = SSM_CHUNK * SSD_SUB
    blk = lambda w: pl.BlockSpec((1, ch, w), lambda bi, c: (bi, c, 0))
    per_chunk = pl.BlockSpec((1, SSD_SUB, SSM_HEADS, SSM_CHUNK), lambda bi, c: (bi, c, 0, 0))
    spread = _head_lane_matrix(SSM_HEADS, LANES)
    masks = _group_head_masks()
    return pl.pallas_call(
        _ssd_kernel,
        grid=(b, s // ch),
        in_specs=[blk(CONV_DIM), blk(SSM_INNER), blk(LANES), per_chunk, per_chunk,
                  _const_spec(spread.shape), _const_spec(masks.shape),
                  _const_spec((1, SSM_INNER)), _const_spec((1, SSM_INNER))],
        out_specs=blk(SSM_INNER),
        out_shape=jax.ShapeDtypeStruct((b, s, SSM_INNER), BF16),
        scratch_shapes=[pltpu.VMEM((SSM_GROUPS, SSM_STATE, GROUP_WIDTH), F32)],
        compiler_params=_params(2),
        name="ssd",
    )(xc, zs, a_cs, a_src_t, w_t, spread, masks, d_skip_wide, norm_w)


_ATT_SLABS = ATT_WIDTH // LANES


def _merge_kernel(x_ref, o1_ref, o4_ref, o16_ref, s1_ref, s4_ref, s16_ref, y_ref, gate_ref,
                  expand_ref, watt_ref, wssm_ref, wout_ref, nw_ref, pre_ref, h_ref, f_ref,
                  onat_ref, snat_ref):
    tm = ROW_TILE
    ssm = jnp.dot(y_ref[...], wssm_ref[...], preferred_element_type=F32)
    for r in range(4):
        snat_ref[0, pl.ds(r, tm // 4, stride=4), :] = s4_ref[0, r]
    for r in range(16):
        snat_ref[1, pl.ds(r, tm // 16, stride=16), :] = s16_ref[0, r]
    for slab in range(_ATT_SLABS):
        cols = slice(slab * LANES, (slab + 1) * LANES)
        for r in range(4):
            onat_ref[0, slab, pl.ds(r, tm // 4, stride=4), :] = o4_ref[0, r, :, cols].astype(F32)
        for r in range(16):
            onat_ref[1, slab, pl.ds(r, tm // 16, stride=16), :] = o16_ref[0, r, :, cols].astype(F32)

    stats = [s1_ref[...], snat_ref[0], snat_ref[1]]
    top = jnp.maximum(jnp.maximum(stats[0], stats[1]), stats[2])
    scale = [jnp.exp2(st - top) for st in stats]
    dens = [pltpu.roll(st, LANES - DEN_LANE, axis=1) for st in stats]
    total = scale[0] * dens[0] + scale[1] * dens[1] + scale[2] * dens[2]
    is_head = lax.broadcasted_iota(jnp.int32, (tm, LANES), 1) < N_ATT_HEADS
    inv = 1.0 / jnp.where(is_head, total, 1.0)
    wts = [jnp.where(is_head, sc * inv, 0.0).astype(BF16) for sc in scale]
    slabs = []
    for slab in range(_ATT_SLABS):
        cols = slice(slab * LANES, (slab + 1) * LANES)
        wide = [jnp.dot(w, expand_ref[:, cols], preferred_element_type=F32) for w in wts]
        slabs.append(wide[0] * o1_ref[:, cols].astype(F32) + wide[1] * onat_ref[0, slab]
                     + wide[2] * onat_ref[1, slab])
    att = jnp.concatenate(slabs, axis=1).astype(BF16)
    att = jnp.dot(att, watt_ref[...], preferred_element_type=F32)
    gates = gate_ref[...].astype(F32)
    mixed = gates[:, :D_MODEL] * att + gates[:, D_MODEL:] * ssm
    mixed = jnp.dot(mixed.astype(BF16), wout_ref[...], preferred_element_type=F32)
    h = x_ref[...] + mixed * _rms_scale(mixed) * nw_ref[...]
    h_ref[...] = h
    f_ref[...] = (h * _rms_scale(h) * pre_ref[...]).astype(BF16)


def _merge(x2, outs, stats, y, gates, expand, w_att, w_ssm, w_out, norm_w, ffn_pre_w):
    t = x2.shape[0]
    tm = ROW_TILE
    row = lambda width: pl.BlockSpec((tm, width), lambda i: (i, 0))
    return pl.pallas_call(
        _merge_kernel,
        grid=(t // tm,),
        in_specs=[row(D_MODEL), row(ATT_WIDTH), _residue4_spec(ATT_WIDTH), _residue16_spec(ATT_WIDTH),
                  row(LANES), _residue4_spec(LANES), _residue16_spec(LANES),
                  row(SSM_INNER), row(2 * D_MODEL),
                  _const_spec((LANES, ATT_WIDTH)), _const_spec((ATT_WIDTH, D_MODEL)),
                  _const_spec((SSM_INNER, D_MODEL)), _const_spec((D_MODEL, D_MODEL)),
                  _const_spec((1, D_MODEL)), _const_spec((1, D_MODEL))],
        out_specs=[row(D_MODEL), row(D_MODEL)],
        out_shape=[jax.ShapeDtypeStruct((t, D_MODEL), F32), jax.ShapeDtypeStruct((t, D_MODEL), BF16)],
        scratch_shapes=[pltpu.VMEM((2, _ATT_SLABS, tm, LANES), F32),
                        pltpu.VMEM((2, tm, LANES), F32)],
        compiler_params=_params(1),
        name="merge",
    )(x2, *outs, *stats, y, gates, expand, w_att, w_ssm, w_out, norm_w, ffn_pre_w)


FFN_CHUNK = 1024


def _ffn_kernel(h_ref, f_ref, wup_ref, wdown_ref, post_ref, o_ref):
    acc = jnp.zeros((ROW_TILE, D_MODEL), F32)
    for c in range(0, FFN_HIDDEN, FFN_CHUNK):
        up = jnp.dot(f_ref[...], wup_ref[:, c:c + FFN_CHUNK], preferred_element_type=F32)
        act = jnp.square(jnp.maximum(up, 0.0)).astype(BF16)
        acc = acc + jnp.dot(act, wdown_ref[c:c + FFN_CHUNK, :], preferred_element_type=F32)
    o_ref[...] = h_ref[...] + acc * _rms_scale(acc) * post_ref[...]


def _ffn(h2, f2, w_up, w_down, post_w):
    t = h2.shape[0]
    row = pl.BlockSpec((ROW_TILE, D_MODEL), lambda i: (i, 0))
    return pl.pallas_call(
        _ffn_kernel,
        grid=(t // ROW_TILE,),
        in_specs=[row, row, _const_spec((D_MODEL, FFN_HIDDEN)),
                  _const_spec((FFN_HIDDEN, D_MODEL)), _const_spec((1, D_MODEL))],
        out_specs=row,
        out_shape=jax.ShapeDtypeStruct((t, D_MODEL), F32),
        compiler_params=_params(1),
        name="ffn",
    )(h2, f2, w_up, w_down, post_w)


def _pad_lanes(v):
    return jnp.pad(v.astype(F32), (0, LANES - v.shape[0])).reshape(1, LANES)


def kernel(x, norm_mix_pre_w, w_in, b_gate, conv_w, conv_b, dt_bias, a_log, d_skip, ssm_norm_w,
           w_att_proj, w_ssm_proj, w_out, norm_mix_post_w, norm_ffn_pre_w, w_up, w_down,
           norm_ffn_post_w):
    b, s, d_model = x.shape
    depth = w_in.shape[0]
    assert d_model == D_MODEL and s % SPAN16 == 0
    t = b * s
    expand = _head_lane_matrix(N_ATT_HEADS, LANES)
    h = x.reshape(t, D_MODEL)
    for layer in range(depth):
        w = w_in[layer]
        off_ssm = 3 * ATT_WIDTH
        off_dt = off_ssm + SSM_INNER + CONV_DIM
        w_qkv = w[:, :off_ssm].astype(BF16)
        w_ssm = jnp.concatenate(
            [0.5 * w[:, off_ssm:off_ssm + SSM_INNER], w[:, off_ssm + SSM_INNER:off_dt],
             w[:, off_dt + SSM_HEADS:],
             jnp.pad(w[:, off_dt:off_dt + SSM_HEADS], ((0, 0), (0, LANES - SSM_HEADS)))],
            axis=1).astype(BF16)
        *qkv, u = _qkv_proj(h, norm_mix_pre_w[layer].reshape(1, -1), w_qkv)
        zs, xc, gates, a_cs, a_src_t, w_t = _ssm_proj(
            u, w_ssm, b_gate[layer].reshape(1, -1), _pad_lanes(dt_bias[layer]),
            0.5 * conv_w[layer], 0.5 * conv_b[layer].reshape(1, -1), _pad_lanes(a_log[layer]), s)
        outs, stats = [], []
        for idx, d in enumerate(DILATIONS):
            q, k, v = qkv[idx], qkv[3 + idx], qkv[6 + idx]
            if d == 1:
                q, k, v = [a.reshape(t // SPAN4, 4, ATT_BLOCK, ATT_WIDTH) for a in (q, k, v)]
            o, st = _attention(q, k, v, d, b)
            if d == 1:
                o, st = o.reshape(t, ATT_WIDTH), st.reshape(t, LANES)
            outs.append(o)
            stats.append(st)
        seq = lambda a: a.reshape(b, s, a.shape[-1])
        per_chunk = lambda a: a.reshape(b, s // SSM_CHUNK, SSM_HEADS, SSM_CHUNK)
        y = _ssd(seq(xc), seq(zs), seq(a_cs), per_chunk(a_src_t), per_chunk(w_t),
                 jnp.repeat(d_skip[layer].astype(F32), HEAD_DIM).reshape(1, -1),
                 ssm_norm_w[layer].reshape(1, -1))
        h, f = _merge(h, outs, stats, y.reshape(t, SSM_INNER), gates, expand,
                      w_att_proj[layer].astype(BF16), w_ssm_proj[layer].astype(BF16),
                      w_out[layer].astype(BF16), norm_mix_post_w[layer].reshape(1, -1),
                      norm_ffn_pre_w[layer].reshape(1, -1))
        h = _ffn(h, f, w_up[layer].astype(BF16), w_down[layer].astype(BF16),
                 norm_ffn_post_w[layer].reshape(1, -1))
    return h.reshape(b, s, D_MODEL)
```

```python
import functools
import math

import numpy as np
import jax
import jax.numpy as jnp
from jax import lax
from jax.experimental import pallas as pl
from jax.experimental.pallas import tpu as pltpu

F32 = jnp.float32
BF16 = jnp.bfloat16

D_MODEL = 1024
HEAD_DIM = 64
N_ATT_HEADS = 12
ATT_WIDTH = N_ATT_HEADS * HEAD_DIM
DILATIONS = (1, 4, 16)
ATT_BLOCK = 128
SSM_INNER = 2048
SSM_HEADS = 32
SSM_GROUPS = 8
SSM_STATE = 128
SSM_CONV = 4
SSM_CHUNK = 128
CONV_DIM = SSM_INNER + 2 * SSM_GROUPS * SSM_STATE
FFN_HIDDEN = 4 * D_MODEL
RMS_EPS = 1e-6
NEG = -1e30

LANES = 128
MXU_N = 256
HEADS_PER_TILE = LANES // HEAD_DIM
N_HEAD_TILES = N_ATT_HEADS // HEADS_PER_TILE
GROUP_WIDTH = SSM_INNER // SSM_GROUPS
HEADS_PER_GROUP = SSM_HEADS // SSM_GROUPS
VMEM_LIMIT_BYTES = 56 * 1024 * 1024

ROW_TILE = 512
SPAN4 = ATT_BLOCK * 4
SPAN16 = ATT_BLOCK * 16
assert ROW_TILE == SPAN4


def _const_spec(shape):
    return pl.BlockSpec(shape, lambda *_: (0,) * len(shape), pipeline_mode=pl.Buffered(1))


def _params(n_axes):
    return pltpu.CompilerParams(dimension_semantics=("arbitrary",) * n_axes,
                                vmem_limit_bytes=VMEM_LIMIT_BYTES)


def _rms_scale(x):
    return lax.rsqrt(jnp.mean(x * x, axis=-1, keepdims=True) + RMS_EPS)


def _residue16_spec(width):
    per_span = SPAN16 // ROW_TILE
    return pl.BlockSpec((1, 16, ROW_TILE // 16, width), lambda i: (i // per_span, 0, i % per_span, 0))


def _residue4_spec(width):
    return pl.BlockSpec((1, 4, ATT_BLOCK, width), lambda i: (i, 0, 0, 0))


_QKV_SLABS = 3 * ATT_WIDTH // LANES
_LOG2E = 1.4426950408889634
Q_SCALE = HEAD_DIM ** -0.5 * _LOG2E


def _qkv_kernel(x_ref, nw_ref, w_ref, *refs):
    out_refs, (u_ref, slab_ref, res4_ref) = refs[:9], refs[9:]
    tm = ROW_TILE
    x = x_ref[...]
    u_ref[...] = (x * _rms_scale(x) * nw_ref[...]).astype(BF16)
    for a in range(3):
        nat_ref, d4_ref, d16_ref = out_refs[3 * a:3 * a + 3]
        for c in range(0, ATT_WIDTH, MXU_N):
            acc = jnp.dot(u_ref[...], w_ref[:, a * ATT_WIDTH + c:a * ATT_WIDTH + c + MXU_N],
                          preferred_element_type=F32)
            if a == 0:
                acc = acc * Q_SCALE
            nat_ref[:, c:c + MXU_N] = acc.astype(BF16)
            for half in range(MXU_N // LANES):
                cols = slice(c + half * LANES, c + (half + 1) * LANES)
                slab = (a * ATT_WIDTH + c) // LANES + half
                slab_ref[slab] = acc[:, half * LANES:(half + 1) * LANES]
                for r in range(4):
                    part = slab_ref[slab, pl.ds(r, tm // 4, stride=4), :]
                    d4_ref[0, r, :, cols] = part.astype(BF16)
                    res4_ref[slab, r] = part
                for r in range(16):
                    part = res4_ref[slab, r % 4, pl.ds(r // 4, tm // 16, stride=4), :]
                    d16_ref[0, r, :, cols] = part.astype(BF16)


def _qkv_proj(x2, norm_w, w_qkv):
    t = x2.shape[0]
    tm = ROW_TILE
    nat = pl.BlockSpec((tm, ATT_WIDTH), lambda i: (i, 0))
    shapes = [jax.ShapeDtypeStruct((t, ATT_WIDTH), BF16),
              jax.ShapeDtypeStruct((t // SPAN4, 4, ATT_BLOCK, ATT_WIDTH), BF16),
              jax.ShapeDtypeStruct((t // SPAN16, 16, ATT_BLOCK, ATT_WIDTH), BF16)]
    return pl.pallas_call(
        _qkv_kernel,
        grid=(t // tm,),
        in_specs=[pl.BlockSpec((tm, D_MODEL), lambda i: (i, 0)), _const_spec((1, D_MODEL)),
                  _const_spec((D_MODEL, 3 * ATT_WIDTH))],
        out_specs=[nat, _residue4_spec(ATT_WIDTH), _residue16_spec(ATT_WIDTH)] * 3
                  + [pl.BlockSpec((tm, D_MODEL), lambda i: (i, 0))],
        out_shape=shapes * 3 + [jax.ShapeDtypeStruct((t, D_MODEL), BF16)],
        scratch_shapes=[pltpu.VMEM((_QKV_SLABS, tm, LANES), F32),
                        pltpu.VMEM((_QKV_SLABS, 4, tm // 4, LANES), F32)],
        compiler_params=_params(1),
        name="qkv_proj",
    )(x2, norm_w, w_qkv)


_HALO = 8
_PAD_BUFFERS = 8
PROJ_N = 1024


def _split3(x):
    hi = x.astype(BF16)
    r = x - hi.astype(F32)
    mid = r.astype(BF16)
    lo = (r - mid.astype(F32)).astype(BF16)
    return hi, mid, lo


def _chunk_tri_matrix():
    i = np.arange(ROW_TILE)
    same_chunk = (i[:, None] // SSM_CHUNK) == (i[None, :] // SSM_CHUNK)
    return jnp.asarray(same_chunk & (i[:, None] >= i[None, :]), dtype=BF16)


def _ssm_proj_kernel(u_ref, wz_ref, wx_ref, wg_ref, wdt_ref, bg_ref, dtb_ref, cw_ref, cb_ref,
                     alog_ref, tri_ref, zs_ref, xc_ref, gate_ref, acs_ref, asrc_ref, wt_ref,
                     halo_ref, pad_ref, *, tiles_per_seq):
    tm = ROW_TILE

    @pl.when(pl.program_id(0) % tiles_per_seq == 0)
    def _():
        halo_ref[...] = jnp.zeros_like(halo_ref)

    def proj(w_ref, off, width):
        return jnp.dot(u_ref[...], w_ref[:, off:off + width], preferred_element_type=F32)

    dt_raw = proj(wdt_ref, 0, LANES) + dtb_ref[...]
    dt = jnp.maximum(dt_raw, 0.0) + jnp.log1p(jnp.exp(-jnp.abs(dt_raw)))
    la = dt * (-_LOG2E * jnp.exp(alog_ref[...]))
    a_cs = sum(jnp.dot(tri_ref[...], part, preferred_element_type=F32) for part in _split3(la))
    acs_ref[...] = a_cs
    ch = SSM_CHUNK
    for cidx in range(tm // ch):
        a_t = a_cs[cidx * ch:(cidx + 1) * ch].T
        dt_t = dt[cidx * ch:(cidx + 1) * ch].T
        asrc_ref[cidx] = (a_t - jnp.log2(dt_t))[:SSM_HEADS]
        wt_ref[cidx] = (jnp.exp2(a_t[:, ch - 1:ch] - a_t) * dt_t)[:SSM_HEADS]

    def z_chunk(c):
        h = proj(wz_ref, c, PROJ_N)
        zs_ref[:, c:c + PROJ_N] = (h + h * jnp.tanh(h)).astype(BF16)

    def gate_chunk(c):
        logits = proj(wg_ref, c, PROJ_N) + bg_ref[:, c:c + PROJ_N]
        gate_ref[:, c:c + PROJ_N] = jax.nn.sigmoid(logits).astype(BF16)

    def conv_chunk(idx):
        acc2 = proj(wx_ref, idx * PROJ_N, PROJ_N)
        for part in range(PROJ_N // LANES):
            c = idx * PROJ_N + part * LANES
            cs = slice(c, c + LANES)
            buf = (c // LANES) % _PAD_BUFFERS
            acc = acc2[:, part * LANES:(part + 1) * LANES]
            pad_ref[buf, pl.ds(0, _HALO, stride=2), :] = halo_ref[:, cs]
            pad_ref[buf, pl.ds(2 * _HALO, tm, stride=2), :] = acc
            halo_ref[:, cs] = acc[tm - _HALO:, :]
            h = cb_ref[:, cs] + cw_ref[SSM_CONV - 1:SSM_CONV, cs] * acc
            for tap in range(SSM_CONV - 1):
                start = 2 * (_HALO - (SSM_CONV - 1) + tap)
                h = h + cw_ref[tap:tap + 1, cs] * pad_ref[buf, pl.ds(start, tm, stride=2), :]
            xc_ref[:, cs] = (h + h * jnp.tanh(h)).astype(BF16)

    light = ([functools.partial(z_chunk, c) for c in range(0, SSM_INNER, PROJ_N)]
             + [functools.partial(gate_chunk, c) for c in range(0, 2 * D_MODEL, PROJ_N)])
    for idx in range(CONV_DIM // PROJ_N):
        conv_chunk(idx)
        light[idx]()


def _ssm_proj(u, w_z_half, w_xbc, w_gate, w_dt_pad, b_gate, dt_bias_pad, conv_w_half, conv_b_half,
              a_log_pad, seq_len):
    t = u.shape[0]
    tm = ROW_TILE
    row = lambda width: pl.BlockSpec((tm, width), lambda i: (i, 0))
    per_chunk = pl.BlockSpec((tm // SSM_CHUNK, SSM_HEADS, SSM_CHUNK), lambda i: (i, 0, 0))
    widths = (SSM_INNER, CONV_DIM, 2 * D_MODEL)
    out_shape = [jax.ShapeDtypeStruct((t, w), BF16) for w in widths]
    out_shape.append(jax.ShapeDtypeStruct((t, LANES), F32))
    out_shape += [jax.ShapeDtypeStruct((t // SSM_CHUNK, SSM_HEADS, SSM_CHUNK), F32)] * 2
    tri = _chunk_tri_matrix()
    return pl.pallas_call(
        functools.partial(_ssm_proj_kernel, tiles_per_seq=seq_len // tm),
        grid=(t // tm,),
        in_specs=[row(D_MODEL), _const_spec((D_MODEL, SSM_INNER)), _const_spec((D_MODEL, CONV_DIM)),
                  _const_spec((D_MODEL, 2 * D_MODEL)), _const_spec((D_MODEL, LANES)),
                  _const_spec((1, 2 * D_MODEL)), _const_spec((1, LANES)),
                  _const_spec((SSM_CONV, CONV_DIM)), _const_spec((1, CONV_DIM)),
                  _const_spec((1, LANES)), _const_spec(tri.shape)],
        out_specs=[row(w) for w in widths] + [row(LANES), per_chunk, per_chunk],
        out_shape=out_shape,
        scratch_shapes=[pltpu.VMEM((_HALO, CONV_DIM), F32),
                        pltpu.VMEM((_PAD_BUFFERS, 2 * (_HALO + tm), LANES), F32)],
        compiler_params=_params(1),
        name="ssm_proj",
    )(u, w_z_half, w_xbc, w_gate, w_dt_pad, b_gate, dt_bias_pad, conv_w_half, conv_b_half,
      a_log_pad, tri)


ATT_SUB = 4


def _alibi_slopes(n):
    def pow2(m):
        start = 2.0 ** (-8.0 / m)
        return [start ** (i + 1) for i in range(m)]
    if (n & (n - 1)) == 0:
        s = pow2(n)
    else:
        c = 2 ** int(math.floor(math.log2(n)))
        s = pow2(c) + pow2(2 * c)[0::2][: n - c]
    return np.array(s, dtype=np.float32)


def _attention_bias(dilation):
    blk = ATT_BLOCK
    i = np.arange(blk)[:, None]
    j = np.arange(blk)[None, :]
    dist_prev = blk + i - j
    dist_cur = i - j
    slopes = (_alibi_slopes(N_ATT_HEADS) * np.float32(_LOG2E)).reshape(N_HEAD_TILES, HEADS_PER_TILE)
    out = np.empty((2, N_HEAD_TILES, HEADS_PER_TILE * blk, 2 * blk), np.float32)
    for tile in range(N_HEAD_TILES):
        for hh in range(HEADS_PER_TILE):
            s = slopes[tile, hh]
            prev = np.where(dist_prev <= blk, -s * (dist_prev * dilation).astype(np.float32), NEG)
            cur = np.where(dist_cur >= 0, -s * (dist_cur * dilation).astype(np.float32), NEG)
            rows = slice(hh * blk, (hh + 1) * blk)
            out[1, tile, rows, :blk] = prev
            out[0, tile, rows, :blk] = NEG
            out[:, tile, rows, blk:] = cur
    return jnp.asarray(out)


DEN_LANE = 16
assert N_ATT_HEADS <= DEN_LANE


def _attention_kernel(q_ref, kc_ref, kp_ref, vc_ref, vp_ref, bias_ref, o_ref, st_ref, *, dilation):
    blk = ATT_BLOCK
    n = pl.program_id(1)
    lane = lax.broadcasted_iota(jnp.int32, (blk, LANES), 1)
    in_head_a = lane < HEAD_DIM
    keep_a = jnp.where(in_head_a, 1.0, 0.0).astype(BF16)
    keep_b = jnp.where(in_head_a, 0.0, 1.0).astype(BF16)
    ones = jnp.ones((2 * blk, LANES), BF16)

    for g in range(ATT_SUB):
        if dilation == 1 and g > 0:
            kprev_ref, vprev_ref, gp, has_prev = kc_ref, vc_ref, g - 1, None
        elif dilation == 1:
            kprev_ref, vprev_ref, gp, has_prev = kp_ref, vp_ref, 0, n > 0
        else:
            kprev_ref, vprev_ref, gp, has_prev = kp_ref, vp_ref, g, n > 0
        stats = jnp.zeros((blk, LANES), F32)
        for tile in range(N_HEAD_TILES):
            cols = slice(tile * LANES, (tile + 1) * LANES)
            q = q_ref[0, g, :, cols]
            q2 = jnp.concatenate([q * keep_a, q * keep_b], axis=0)
            k2 = jnp.concatenate([kprev_ref[0, gp, :, cols], kc_ref[0, g, :, cols]], axis=0)
            s = lax.dot_general(q2, k2, (((1,), (1,)), ((), ())), preferred_element_type=F32)
            if has_prev is None:
                s = s + bias_ref[1, tile]
            else:
                s = s + bias_ref[has_prev.astype(jnp.int32), tile]
            m = jnp.max(s, axis=-1, keepdims=True)
            p = jnp.exp2(s - m).astype(BF16)
            v2 = jnp.concatenate([vprev_ref[0, gp, :, cols], vc_ref[0, g, :, cols]], axis=0)
            r = jnp.dot(p, jnp.concatenate([v2, ones], axis=1), preferred_element_type=F32)
            acc, den = r[:, :LANES], r[:, LANES:]
            o_ref[0, g, :, cols] = jnp.where(in_head_a, acc[:blk], acc[blk:]).astype(BF16)
            for hh in range(HEADS_PER_TILE):
                head = tile * HEADS_PER_TILE + hh
                stats = jnp.where(lane == head, m[hh * blk:(hh + 1) * blk], stats)
                stats = jnp.where(lane == DEN_LANE + head, den[hh * blk:(hh + 1) * blk], stats)
        st_ref[0, g] = stats


def _attention(q, k, v, dilation, batch):
    spans, residues, blk, _ = q.shape
    per_batch = spans // batch
    d = dilation
    if d == 16:
        grid = (batch, per_batch, residues // ATT_SUB)
        cur_map = lambda bi, n, rg: (bi * per_batch + n, rg, 0, 0)
        prev_map = lambda bi, n, rg: (bi * per_batch + jnp.maximum(n - 1, 0), rg, 0, 0)
        prev_sub = ATT_SUB
    else:
        grid = (batch, per_batch)
        cur_map = lambda bi, n: (bi * per_batch + n, 0, 0, 0)
        if d == 1:
            prev_map = lambda bi, n: (bi * per_batch + jnp.maximum(n - 1, 0), ATT_SUB - 1, 0, 0)
            prev_sub = 1
        else:
            prev_map = lambda bi, n: (bi * per_batch + jnp.maximum(n - 1, 0), 0, 0, 0)
            prev_sub = ATT_SUB
    cur = lambda w: pl.BlockSpec((1, ATT_SUB, blk, w), cur_map)
    prev = lambda w: pl.BlockSpec((1, prev_sub, blk, w), prev_map)
    bias = _attention_bias(d)
    return pl.pallas_call(
        functools.partial(_attention_kernel, dilation=d),
        grid=grid,
        in_specs=[cur(ATT_WIDTH), cur(ATT_WIDTH), prev(ATT_WIDTH), cur(ATT_WIDTH), prev(ATT_WIDTH),
                  _const_spec(bias.shape)],
        out_specs=[cur(ATT_WIDTH), cur(LANES)],
        out_shape=[jax.ShapeDtypeStruct(q.shape, BF16),
                   jax.ShapeDtypeStruct((spans, residues, blk, LANES), F32)],
        compiler_params=_params(len(grid)),
        name=f"attention_d{d}",
    )(q, k, k, v, v, bias)


SSD_SUB = 4


def _head_lane_matrix(n_heads, rows):
    e = np.zeros((rows, n_heads * HEAD_DIM), np.float32)
    for h in range(n_heads):
        e[h, h * HEAD_DIM:(h + 1) * HEAD_DIM] = 1.0
    return jnp.asarray(e, dtype=BF16)


def _group_head_masks():
    m = np.zeros((HEADS_PER_GROUP, SSM_CHUNK, GROUP_WIDTH), np.float32)
    for hh in range(HEADS_PER_GROUP):
        m[hh, :, hh * HEAD_DIM:(hh + 1) * HEAD_DIM] = 1.0
    return jnp.asarray(m, dtype=BF16)


def _ssd_kernel(xc_ref, zs_ref, acs_ref, asrc_ref, wt_ref, spread_ref, mask_ref, dskip_ref, nw_ref,
                y_ref, st_ref):
    @pl.when(pl.program_id(1) == 0)
    def _():
        st_ref[...] = jnp.zeros_like(st_ref)

    for sub in range(SSD_SUB):
        _ssd_chunk(sub, xc_ref, zs_ref, acs_ref, asrc_ref, wt_ref, spread_ref, mask_ref, dskip_ref,
                   nw_ref, y_ref, st_ref)


def _ssd_chunk(sub, xc_ref, zs_ref, acs_ref, asrc_ref, wt_ref, spread_ref, mask_ref, dskip_ref,
               nw_ref, y_ref, st_ref):
    ch = SSM_CHUNK
    rows = slice(sub * ch, (sub + 1) * ch)
    a_cs = acs_ref[0, rows, :]
    a_src_t = asrc_ref[0, sub]
    w_bf = wt_ref[0, sub].astype(BF16)
    row_i = lax.broadcasted_iota(jnp.int32, (ch, ch), 0)
    col_i = lax.broadcasted_iota(jnp.int32, (ch, ch), 1)
    causal = row_i >= col_i
    e = jnp.exp2(a_cs)
    e_hi = e.astype(BF16)
    e_lo = (e - e_hi.astype(F32)).astype(BF16)
    a_exp_all = (jnp.dot(e_hi, spread_ref[...], preferred_element_type=F32)
                 + jnp.dot(e_lo, spread_ref[...], preferred_element_type=F32))

    for g in range(SSM_GROUPS):
        gs = slice(g * GROUP_WIDTH, (g + 1) * GROUP_WIDTH)
        b_bf = xc_ref[0, rows, SSM_INNER + g * SSM_STATE:SSM_INNER + (g + 1) * SSM_STATE]
        c_bf = xc_ref[0, rows, SSM_INNER + (SSM_GROUPS + g) * SSM_STATE:
                      SSM_INNER + (SSM_GROUPS + g + 1) * SSM_STATE]
        cb = lax.dot_general(c_bf, b_bf, (((1,), (1,)), ((), ())),
                             preferred_element_type=F32).astype(BF16)
        b_t = b_bf.astype(F32).T.astype(BF16)
        x_bf = xc_ref[0, rows, gs]

        intra, to_state, x_diag = [], [], []
        for hh in range(HEADS_PER_GROUP):
            h = g * HEADS_PER_GROUP + hh
            a_col = jnp.broadcast_to(a_cs[:, h:h + 1], (ch, ch))
            decay_dt = jnp.exp2(jnp.where(causal, a_col - a_src_t[h:h + 1, :], NEG))
            intra.append(cb * decay_dt.astype(BF16))
            to_state.append(b_t * w_bf[h:h + 1, :])
            x_diag.append(x_bf * mask_ref[hh])
        lhs = jnp.concatenate([jnp.concatenate(intra, axis=1), jnp.concatenate(to_state, axis=1)], axis=0)
        r = jnp.dot(lhs, jnp.concatenate(x_diag, axis=0), preferred_element_type=F32)
        y_diag, st_inc = r[:ch], r[ch:]

        a_exp = a_exp_all[:, gs]
        st_old = st_ref[g]
        y_off = jnp.dot(c_bf, st_old.astype(BF16), preferred_element_type=F32) * a_exp
        st_ref[g] = st_old * a_exp[ch - 1:ch, :] + st_inc

        y = y_diag + y_off + dskip_ref[:, gs] * x_bf.astype(F32)
        y = y * zs_ref[0, rows, gs].astype(F32)
        y_ref[0, rows, gs] = (y * _rms_scale(y) * nw_ref[:, gs]).astype(BF16)


def _ssd(xc, zs, a_cs, a_src_t, w_t, d_skip_wide, norm_w):
    b, s, _ = xc.shape
    ch = SSM_CHUNK * SSD_SUB
    blk = lambda w: pl.BlockSpec((1, ch, w), lambda bi, c: (bi, c, 0))
    per_chunk = pl.BlockSpec((1, SSD_SUB, SSM_HEADS, SSM_CHUNK), lambda bi, c: (bi, c, 0, 0))
    spread = _head_lane_matrix(SSM_HEADS, LANES)
    masks = _group_head_masks()
    return pl.pallas_call(
        _ssd_kernel,
        grid=(b, s // ch),
        in_specs=[blk(CONV_DIM), blk(SSM_INNER), blk(LANES), per_chunk, per_chunk,
                  _const_spec(spread.shape), _const_spec(masks.shape),
                  _const_spec((1, SSM_INNER)), _const_spec((1, SSM_INNER))],
        out_specs=blk(SSM_INNER),
        out_shape=jax.ShapeDtypeStruct((b, s, SSM_INNER), BF16),
        scratch_shapes=[pltpu.VMEM((SSM_GROUPS, SSM_STATE, GROUP_WIDTH), F32)],
        compiler_params=_params(2),
        name="ssd",
    )(xc, zs, a_cs, a_src_t, w_t, spread, masks, d_skip_wide, norm_w)


_ATT_SLABS = ATT_WIDTH // LANES


def _merge_kernel(x_ref, o1_ref, o4_ref, o16_ref, s1_ref, s4_ref, s16_ref, y_ref, gate_ref,
                  expand_ref, watt_ref, wssm_ref, wout_ref, nw_ref, pre_ref, h_ref, f_ref,
                  onat_ref, snat_ref):
    tm = ROW_TILE
    ssm = jnp.dot(y_ref[...], wssm_ref[...], preferred_element_type=F32)
    for r in range(4):
        snat_ref[0, pl.ds(r, tm // 4, stride=4), :] = s4_ref[0, r]
    for r in range(16):
        snat_ref[1, pl.ds(r, tm // 16, stride=16), :] = s16_ref[0, r]
    for slab in range(_ATT_SLABS):
        cols = slice(slab * LANES, (slab + 1) * LANES)
        for r in range(4):
            onat_ref[0, slab, pl.ds(r, tm // 4, stride=4), :] = o4_ref[0, r, :, cols].astype(F32)
        for r in range(16):
            onat_ref[1, slab, pl.ds(r, tm // 16, stride=16), :] = o16_ref[0, r, :, cols].astype(F32)

    stats = [s1_ref[...], snat_ref[0], snat_ref[1]]
    top = jnp.maximum(jnp.maximum(stats[0], stats[1]), stats[2])
    scale = [jnp.exp2(st - top) for st in stats]
    dens = [pltpu.roll(st, LANES - DEN_LANE, axis=1) for st in stats]
    total = scale[0] * dens[0] + scale[1] * dens[1] + scale[2] * dens[2]
    is_head = lax.broadcasted_iota(jnp.int32, (tm, LANES), 1) < N_ATT_HEADS
    inv = 1.0 / jnp.where(is_head, total, 1.0)
    wts = [jnp.where(is_head, sc * inv, 0.0).astype(BF16) for sc in scale]
    slabs = []
    for slab in range(_ATT_SLABS):
        cols = slice(slab * LANES, (slab + 1) * LANES)
        wide = [jnp.dot(w, expand_ref[:, cols], preferred_element_type=F32) for w in wts]
        slabs.append(wide[0] * o1_ref[:, cols].astype(F32) + wide[1] * onat_ref[0, slab]
                     + wide[2] * onat_ref[1, slab])
    att = jnp.concatenate(slabs, axis=1).astype(BF16)
    att = jnp.dot(att, watt_ref[...], preferred_element_type=F32)
    gates = gate_ref[...].astype(F32)
    mixed = gates[:, :D_MODEL] * att + gates[:, D_MODEL:] * ssm
    mixed = jnp.dot(mixed.astype(BF16), wout_ref[...], preferred_element_type=F32)
    h = x_ref[...] + mixed * _rms_scale(mixed) * nw_ref[...]
    h_ref[...] = h
    f_ref[...] = (h * _rms_scale(h) * pre_ref[...]).astype(BF16)


def _merge(x2, outs, stats, y, gates, expand, w_att, w_ssm, w_out, norm_w, ffn_pre_w):
    t = x2.shape[0]
    tm = ROW_TILE
    row = lambda width: pl.BlockSpec((tm, width), lambda i: (i, 0))
    return pl.pallas_call(
        _merge_kernel,
        grid=(t // tm,),
        in_specs=[row(D_MODEL), row(ATT_WIDTH), _residue4_spec(ATT_WIDTH), _residue16_spec(ATT_WIDTH),
                  row(LANES), _residue4_spec(LANES), _residue16_spec(LANES),
                  row(SSM_INNER), row(2 * D_MODEL),
                  _const_spec((LANES, ATT_WIDTH)), _const_spec((ATT_WIDTH, D_MODEL)),
                  _const_spec((SSM_INNER, D_MODEL)), _const_spec((D_MODEL, D_MODEL)),
                  _const_spec((1, D_MODEL)), _const_spec((1, D_MODEL))],
        out_specs=[row(D_MODEL), row(D_MODEL)],
        out_shape=[jax.ShapeDtypeStruct((t, D_MODEL), F32), jax.ShapeDtypeStruct((t, D_MODEL), BF16)],
        scratch_shapes=[pltpu.VMEM((2, _ATT_SLABS, tm, LANES), F32),
                        pltpu.VMEM((2, tm, LANES), F32)],
        compiler_params=_params(1),
        name="merge",
    )(x2, *outs, *stats, y, gates, expand, w_att, w_ssm, w_out, norm_w, ffn_pre_w)


FFN_CHUNK = 1024


def _ffn_kernel(h_ref, f_ref, wup_ref, wdown_ref, post_ref, o_ref):
    acc = jnp.zeros((ROW_TILE, D_MODEL), F32)
    for c in range(0, FFN_HIDDEN, FFN_CHUNK):
        up = jnp.dot(f_ref[...], wup_ref[:, c:c + FFN_CHUNK], preferred_element_type=F32)
        act = jnp.square(jnp.maximum(up, 0.0)).astype(BF16)
        acc = acc + jnp.dot(act, wdown_ref[c:c + FFN_CHUNK, :], preferred_element_type=F32)
    o_ref[...] = h_ref[...] + acc * _rms_scale(acc) * post_ref[...]


def _ffn(h2, f2, w_up, w_down, post_w):
    t = h2.shape[0]
    row = pl.BlockSpec((ROW_TILE, D_MODEL), lambda i: (i, 0))
    return pl.pallas_call(
        _ffn_kernel,
        grid=(t // ROW_TILE,),
        in_specs=[row, row, _const_spec((D_MODEL, FFN_HIDDEN)),
                  _const_spec((FFN_HIDDEN, D_MODEL)), _const_spec((1, D_MODEL))],
        out_specs=row,
        out_shape=jax.ShapeDtypeStruct((t, D_MODEL), F32),
        compiler_params=_params(1),
        name="ffn",
    )(h2, f2, w_up, w_down, post_w)


def _pad_lanes(v):
    return jnp.pad(v.astype(F32), (0, LANES - v.shape[0])).reshape(1, LANES)


def kernel(x, norm_mix_pre_w, w_in, b_gate, conv_w, conv_b, dt_bias, a_log, d_skip, ssm_norm_w,
           w_att_proj, w_ssm_proj, w_out, norm_mix_post_w, norm_ffn_pre_w, w_up, w_down,
           norm_ffn_post_w):
    b, s, d_model = x.shape
    depth = w_in.shape[0]
    assert d_model == D_MODEL and s % SPAN16 == 0
    t = b * s
    expand = _head_lane_matrix(N_ATT_HEADS, LANES)
    h = x.reshape(t, D_MODEL)
    for layer in range(depth):
        w = w_in[layer]
        off_ssm = 3 * ATT_WIDTH
        off_dt = off_ssm + SSM_INNER + CONV_DIM
        w_qkv = w[:, :off_ssm].astype(BF16)
        off_xbc = off_ssm + SSM_INNER
        w_z_half = (0.5 * w[:, off_ssm:off_xbc]).astype(BF16)
        w_xbc = w[:, off_xbc:off_dt].astype(BF16)
        w_gate = w[:, off_dt + SSM_HEADS:].astype(BF16)
        w_dt_pad = jnp.pad(w[:, off_dt:off_dt + SSM_HEADS],
                           ((0, 0), (0, LANES - SSM_HEADS))).astype(BF16)
        *qkv, u = _qkv_proj(h, norm_mix_pre_w[layer].reshape(1, -1), w_qkv)
        zs, xc, gates, a_cs, a_src_t, w_t = _ssm_proj(
            u, w_z_half, w_xbc, w_gate, w_dt_pad, b_gate[layer].reshape(1, -1),
            _pad_lanes(dt_bias[layer]),
            0.5 * conv_w[layer], 0.5 * conv_b[layer].reshape(1, -1), _pad_lanes(a_log[layer]), s)
        outs, stats = [], []
        for idx, d in enumerate(DILATIONS):
            q, k, v = qkv[idx], qkv[3 + idx], qkv[6 + idx]
            if d == 1:
                q, k, v = [a.reshape(t // SPAN4, 4, ATT_BLOCK, ATT_WIDTH) for a in (q, k, v)]
            o, st = _attention(q, k, v, d, b)
            if d == 1:
                o, st = o.reshape(t, ATT_WIDTH), st.reshape(t, LANES)
            outs.append(o)
            stats.append(st)
        seq = lambda a: a.reshape(b, s, a.shape[-1])
        per_chunk = lambda a: a.reshape(b, s // SSM_CHUNK, SSM_HEADS, SSM_CHUNK)
        y = _ssd(seq(xc), seq(zs), seq(a_cs), per_chunk(a_src_t), per_chunk(w_t),
                 jnp.repeat(d_skip[layer].astype(F32), HEAD_DIM).reshape(1, -1),
                 ssm_norm_w[layer].reshape(1, -1))
        h, f = _merge(h, outs, stats, y.reshape(t, SSM_INNER), gates, expand,
                      w_att_proj[layer].astype(BF16), w_ssm_proj[layer].astype(BF16),
                      w_out[layer].astype(BF16), norm_mix_post_w[layer].reshape(1, -1),
                      norm_ffn_pre_w[layer].reshape(1, -1))
        h = _ffn(h, f, w_up[layer].astype(BF16), w_down[layer].astype(BF16),
                 norm_ffn_post_w[layer].reshape(1, -1))
    return h.reshape(b, s, D_MODEL)
```

```python
import functools
import math

import numpy as np
import jax
import jax.numpy as jnp
from jax import lax
from jax.experimental import pallas as pl
from jax.experimental.pallas import tpu as pltpu

F32 = jnp.float32
BF16 = jnp.bfloat16

D_MODEL = 1024
HEAD_DIM = 64
N_ATT_HEADS = 12
ATT_WIDTH = N_ATT_HEADS * HEAD_DIM
DILATIONS = (1, 4, 16)
ATT_BLOCK = 128
SSM_INNER = 2048
SSM_HEADS = 32
SSM_GROUPS = 8
SSM_STATE = 128
SSM_CONV = 4
SSM_CHUNK = 128
CONV_DIM = SSM_INNER + 2 * SSM_GROUPS * SSM_STATE
FFN_HIDDEN = 4 * D_MODEL
RMS_EPS = 1e-6
NEG = -1e30

LANES = 128
MXU_N = 256
HEADS_PER_TILE = LANES // HEAD_DIM
N_HEAD_TILES = N_ATT_HEADS // HEADS_PER_TILE
GROUP_WIDTH = SSM_INNER // SSM_GROUPS
HEADS_PER_GROUP = SSM_HEADS // SSM_GROUPS
VMEM_LIMIT_BYTES = 56 * 1024 * 1024

ROW_TILE = 512
SPAN4 = ATT_BLOCK * 4
SPAN16 = ATT_BLOCK * 16
assert ROW_TILE == SPAN4


def _const_spec(shape):
    return pl.BlockSpec(shape, lambda *_: (0,) * len(shape), pipeline_mode=pl.Buffered(1))


def _params(n_axes):
    return pltpu.CompilerParams(dimension_semantics=("arbitrary",) * n_axes,
                                vmem_limit_bytes=VMEM_LIMIT_BYTES)


def _rms_scale(x):
    return lax.rsqrt(jnp.mean(x * x, axis=-1, keepdims=True) + RMS_EPS)


def _residue16_spec(width):
    per_span = SPAN16 // ROW_TILE
    return pl.BlockSpec((1, 16, ROW_TILE // 16, width), lambda i: (i // per_span, 0, i % per_span, 0))


def _residue4_spec(width):
    return pl.BlockSpec((1, 4, ATT_BLOCK, width), lambda i: (i, 0, 0, 0))


_QKV_SLABS = 3 * ATT_WIDTH // LANES
_LOG2E = 1.4426950408889634
Q_SCALE = HEAD_DIM ** -0.5 * _LOG2E


def _qkv_kernel(x_ref, nw_ref, w_ref, *refs):
    out_refs, (u_ref, slab_ref, res4_ref) = refs[:6], refs[6:]
    tm = ROW_TILE
    x = x_ref[...]
    u_ref[...] = (x * _rms_scale(x) * nw_ref[...]).astype(BF16)
    for a in range(3):
        d4_ref, d16_ref = out_refs[2 * a:2 * a + 2]
        for c in range(0, ATT_WIDTH, MXU_N):
            acc = jnp.dot(u_ref[...], w_ref[:, a * ATT_WIDTH + c:a * ATT_WIDTH + c + MXU_N],
                          preferred_element_type=F32)
            if a == 0:
                acc = acc * Q_SCALE
            for half in range(MXU_N // LANES):
                cols = slice(c + half * LANES, c + (half + 1) * LANES)
                slab = (a * ATT_WIDTH + c) // LANES + half
                slab_ref[slab] = acc[:, half * LANES:(half + 1) * LANES]
                for r in range(4):
                    part = slab_ref[slab, pl.ds(r, tm // 4, stride=4), :]
                    d4_ref[0, r, :, cols] = part.astype(BF16)
                    res4_ref[slab, r] = part
                for r in range(16):
                    part = res4_ref[slab, r % 4, pl.ds(r // 4, tm // 16, stride=4), :]
                    d16_ref[0, r, :, cols] = part.astype(BF16)


def _qkv_proj(x2, norm_w, w_qkv):
    t = x2.shape[0]
    tm = ROW_TILE
    shapes = [jax.ShapeDtypeStruct((t // SPAN4, 4, ATT_BLOCK, ATT_WIDTH), BF16),
              jax.ShapeDtypeStruct((t // SPAN16, 16, ATT_BLOCK, ATT_WIDTH), BF16)]
    return pl.pallas_call(
        _qkv_kernel,
        grid=(t // tm,),
        in_specs=[pl.BlockSpec((tm, D_MODEL), lambda i: (i, 0)), _const_spec((1, D_MODEL)),
                  _const_spec((D_MODEL, 3 * ATT_WIDTH))],
        out_specs=[_residue4_spec(ATT_WIDTH), _residue16_spec(ATT_WIDTH)] * 3
                  + [pl.BlockSpec((tm, D_MODEL), lambda i: (i, 0))],
        out_shape=shapes * 3 + [jax.ShapeDtypeStruct((t, D_MODEL), BF16)],
        scratch_shapes=[pltpu.VMEM((_QKV_SLABS, tm, LANES), F32),
                        pltpu.VMEM((_QKV_SLABS, 4, tm // 4, LANES), F32)],
        compiler_params=_params(1),
        name="qkv_proj",
    )(x2, norm_w, w_qkv)


_HALO = 8
_PAD_BUFFERS = 8
PROJ_N = 1024


def _split3(x):
    hi = x.astype(BF16)
    r = x - hi.astype(F32)
    mid = r.astype(BF16)
    lo = (r - mid.astype(F32)).astype(BF16)
    return hi, mid, lo


def _chunk_tri_matrix():
    i = np.arange(ROW_TILE)
    same_chunk = (i[:, None] // SSM_CHUNK) == (i[None, :] // SSM_CHUNK)
    return jnp.asarray(same_chunk & (i[:, None] >= i[None, :]), dtype=BF16)


def _ssm_proj_kernel(u_ref, wz_ref, wx_ref, wg_ref, wdt_ref, bg_ref, dtb_ref, cw_ref, cb_ref,
                     alog_ref, tri_ref, zs_ref, xc_ref, gate_ref, acs_ref, asrc_ref, wt_ref,
                     halo_ref, pad_ref, *, tiles_per_seq):
    tm = ROW_TILE

    @pl.when(pl.program_id(0) % tiles_per_seq == 0)
    def _():
        halo_ref[...] = jnp.zeros_like(halo_ref)

    def proj(w_ref, off, width):
        return jnp.dot(u_ref[...], w_ref[:, off:off + width], preferred_element_type=F32)

    dt_raw = proj(wdt_ref, 0, LANES) + dtb_ref[...]
    dt = jnp.maximum(dt_raw, 0.0) + jnp.log1p(jnp.exp(-jnp.abs(dt_raw)))
    la = dt * (-_LOG2E * jnp.exp(alog_ref[...]))
    a_cs = sum(jnp.dot(tri_ref[...], part, preferred_element_type=F32) for part in _split3(la))
    acs_ref[...] = a_cs
    ch = SSM_CHUNK
    for cidx in range(tm // ch):
        a_t = a_cs[cidx * ch:(cidx + 1) * ch].T
        dt_t = dt[cidx * ch:(cidx + 1) * ch].T
        asrc_ref[cidx] = (a_t - jnp.log2(dt_t))[:SSM_HEADS]
        wt_ref[cidx] = (jnp.exp2(a_t[:, ch - 1:ch] - a_t) * dt_t)[:SSM_HEADS]

    def z_chunk(c):
        h = proj(wz_ref, c, PROJ_N)
        zs_ref[:, c:c + PROJ_N] = (h + h * jnp.tanh(h)).astype(BF16)

    def gate_chunk(c):
        logits = proj(wg_ref, c, PROJ_N) + bg_ref[:, c:c + PROJ_N]
        gate_ref[:, c:c + PROJ_N] = jax.nn.sigmoid(logits).astype(BF16)

    def conv_chunk(idx):
        acc2 = proj(wx_ref, idx * PROJ_N, PROJ_N)
        for part in range(PROJ_N // LANES):
            c = idx * PROJ_N + part * LANES
            cs = slice(c, c + LANES)
            buf = (c // LANES) % _PAD_BUFFERS
            acc = acc2[:, part * LANES:(part + 1) * LANES]
            pad_ref[buf, pl.ds(0, _HALO, stride=2), :] = halo_ref[:, cs]
            pad_ref[buf, pl.ds(2 * _HALO, tm, stride=2), :] = acc
            halo_ref[:, cs] = acc[tm - _HALO:, :]
            h = cb_ref[:, cs] + cw_ref[SSM_CONV - 1:SSM_CONV, cs] * acc
            for tap in range(SSM_CONV - 1):
                start = 2 * (_HALO - (SSM_CONV - 1) + tap)
                h = h + cw_ref[tap:tap + 1, cs] * pad_ref[buf, pl.ds(start, tm, stride=2), :]
            xc_ref[:, cs] = (h + h * jnp.tanh(h)).astype(BF16)

    light = ([functools.partial(z_chunk, c) for c in range(0, SSM_INNER, PROJ_N)]
             + [functools.partial(gate_chunk, c) for c in range(0, 2 * D_MODEL, PROJ_N)])
    for idx in range(CONV_DIM // PROJ_N):
        conv_chunk(idx)
        light[idx]()


def _ssm_proj(u, w_z_half, w_xbc, w_gate, w_dt_pad, b_gate, dt_bias_pad, conv_w_half, conv_b_half,
              a_log_pad, seq_len):
    t = u.shape[0]
    tm = ROW_TILE
    row = lambda width: pl.BlockSpec((tm, width), lambda i: (i, 0))
    per_chunk = pl.BlockSpec((tm // SSM_CHUNK, SSM_HEADS, SSM_CHUNK), lambda i: (i, 0, 0))
    widths = (SSM_INNER, CONV_DIM, 2 * D_MODEL)
    out_shape = [jax.ShapeDtypeStruct((t, w), BF16) for w in widths]
    out_shape.append(jax.ShapeDtypeStruct((t, LANES), F32))
    out_shape += [jax.ShapeDtypeStruct((t // SSM_CHUNK, SSM_HEADS, SSM_CHUNK), F32)] * 2
    tri = _chunk_tri_matrix()
    return pl.pallas_call(
        functools.partial(_ssm_proj_kernel, tiles_per_seq=seq_len // tm),
        grid=(t // tm,),
        in_specs=[row(D_MODEL), _const_spec((D_MODEL, SSM_INNER)), _const_spec((D_MODEL, CONV_DIM)),
                  _const_spec((D_MODEL, 2 * D_MODEL)), _const_spec((D_MODEL, LANES)),
                  _const_spec((1, 2 * D_MODEL)), _const_spec((1, LANES)),
                  _const_spec((SSM_CONV, CONV_DIM)), _const_spec((1, CONV_DIM)),
                  _const_spec((1, LANES)), _const_spec(tri.shape)],
        out_specs=[row(w) for w in widths] + [row(LANES), per_chunk, per_chunk],
        out_shape=out_shape,
        scratch_shapes=[pltpu.VMEM((_HALO, CONV_DIM), F32),
                        pltpu.VMEM((_PAD_BUFFERS, 2 * (_HALO + tm), LANES), F32)],
        compiler_params=_params(1),
        name="ssm_proj",
    )(u, w_z_half, w_xbc, w_gate, w_dt_pad, b_gate, dt_bias_pad, conv_w_half, conv_b_half,
      a_log_pad, tri)


ATT_SUB = 4


def _alibi_slopes(n):
    def pow2(m):
        start = 2.0 ** (-8.0 / m)
        return [start ** (i + 1) for i in range(m)]
    if (n & (n - 1)) == 0:
        s = pow2(n)
    else:
        c = 2 ** int(math.floor(math.log2(n)))
        s = pow2(c) + pow2(2 * c)[0::2][: n - c]
    return np.array(s, dtype=np.float32)


def _attention_bias(dilation, interleave):
    blk = ATT_BLOCK
    seg = blk // interleave
    pos = interleave * (np.arange(blk) % seg) + np.arange(blk) // seg
    i = pos[:, None]
    j = pos[None, :]
    dist_prev = blk + i - j
    dist_cur = i - j
    slopes = (_alibi_slopes(N_ATT_HEADS) * np.float32(_LOG2E)).reshape(N_HEAD_TILES, HEADS_PER_TILE)
    out = np.empty((2, N_HEAD_TILES, HEADS_PER_TILE * blk, 2 * blk), np.float32)
    for tile in range(N_HEAD_TILES):
        for hh in range(HEADS_PER_TILE):
            s = slopes[tile, hh]
            prev = np.where(dist_prev <= blk, -s * (dist_prev * dilation).astype(np.float32), NEG)
            cur = np.where(dist_cur >= 0, -s * (dist_cur * dilation).astype(np.float32), NEG)
            rows = slice(hh * blk, (hh + 1) * blk)
            out[1, tile, rows, :blk] = prev
            out[0, tile, rows, :blk] = NEG
            out[:, tile, rows, blk:] = cur
    return jnp.asarray(out)


DEN_LANE = 16
assert N_ATT_HEADS <= DEN_LANE


def _attention_kernel(q_ref, kc_ref, kp_ref, vc_ref, vp_ref, bias_ref, o_ref, st_ref, *, dilation):
    blk = ATT_BLOCK
    seg = blk // 4
    n = pl.program_id(1)
    lane = lax.broadcasted_iota(jnp.int32, (blk, LANES), 1)
    in_head_a = lane < HEAD_DIM
    keep_a = jnp.where(in_head_a, 1.0, 0.0).astype(BF16)
    keep_b = jnp.where(in_head_a, 0.0, 1.0).astype(BF16)
    ones = jnp.ones((2 * blk, LANES), BF16)

    def block(ref, g, cols):
        if dilation == 1:
            return jnp.concatenate([ref[0, r, g * seg:(g + 1) * seg, cols] for r in range(4)], axis=0)
        return ref[0, g, :, cols]

    def store(ref, g, cols, value):
        if dilation == 1:
            for r in range(4):
                ref[0, r, g * seg:(g + 1) * seg, cols] = value[r * seg:(r + 1) * seg]
        else:
            ref[0, g, :, cols] = value

    for g in range(ATT_SUB):
        stats = jnp.zeros((blk, LANES), F32)
        for tile in range(N_HEAD_TILES):
            cols = slice(tile * LANES, (tile + 1) * LANES)
            if dilation == 1 and g > 0:
                k_prev, v_prev, has_prev = block(kc_ref, g - 1, cols), block(vc_ref, g - 1, cols), None
            elif dilation == 1:
                k_prev = jnp.concatenate([kp_ref[0, r, :, cols] for r in range(4)], axis=0)
                v_prev = jnp.concatenate([vp_ref[0, r, :, cols] for r in range(4)], axis=0)
                has_prev = n > 0
            else:
                k_prev, v_prev, has_prev = kp_ref[0, g, :, cols], vp_ref[0, g, :, cols], n > 0
            q = block(q_ref, g, cols)
            q2 = jnp.concatenate([q * keep_a, q * keep_b], axis=0)
            k2 = jnp.concatenate([k_prev, block(kc_ref, g, cols)], axis=0)
            s = lax.dot_general(q2, k2, (((1,), (1,)), ((), ())), preferred_element_type=F32)
            if has_prev is None:
                s = s + bias_ref[1, tile]
            else:
                s = s + bias_ref[has_prev.astype(jnp.int32), tile]
            m = jnp.max(s, axis=-1, keepdims=True)
            p = jnp.exp2(s - m).astype(BF16)
            v2 = jnp.concatenate([v_prev, block(vc_ref, g, cols)], axis=0)
            r = jnp.dot(p, jnp.concatenate([v2, ones], axis=1), preferred_element_type=F32)
            acc, den = r[:, :LANES], r[:, LANES:]
            store(o_ref, g, cols, jnp.where(in_head_a, acc[:blk], acc[blk:]).astype(BF16))
            for hh in range(HEADS_PER_TILE):
                head = tile * HEADS_PER_TILE + hh
                stats = jnp.where(lane == head, m[hh * blk:(hh + 1) * blk], stats)
                stats = jnp.where(lane == DEN_LANE + head, den[hh * blk:(hh + 1) * blk], stats)
        store(st_ref, g, slice(None), stats)


def _attention(q, k, v, dilation, batch):
    spans, residues, blk, _ = q.shape
    per_batch = spans // batch
    d = dilation
    prev_rows = blk
    if d == 16:
        grid = (batch, per_batch, residues // ATT_SUB)
        cur_map = lambda bi, n, rg: (bi * per_batch + n, rg, 0, 0)
        prev_map = lambda bi, n, rg: (bi * per_batch + jnp.maximum(n - 1, 0), rg, 0, 0)
    else:
        grid = (batch, per_batch)
        cur_map = lambda bi, n: (bi * per_batch + n, 0, 0, 0)
        if d == 1:
            prev_rows = blk // 4
            prev_map = lambda bi, n: (bi * per_batch + jnp.maximum(n - 1, 0), 0, 3, 0)
        else:
            prev_map = lambda bi, n: (bi * per_batch + jnp.maximum(n - 1, 0), 0, 0, 0)
    cur = lambda w: pl.BlockSpec((1, ATT_SUB, blk, w), cur_map)
    prev = lambda w: pl.BlockSpec((1, ATT_SUB, prev_rows, w), prev_map)
    bias = _attention_bias(d, 4 if d == 1 else 1)
    return pl.pallas_call(
        functools.partial(_attention_kernel, dilation=d),
        grid=grid,
        in_specs=[cur(ATT_WIDTH), cur(ATT_WIDTH), prev(ATT_WIDTH), cur(ATT_WIDTH), prev(ATT_WIDTH),
                  _const_spec(bias.shape)],
        out_specs=[cur(ATT_WIDTH), cur(LANES)],
        out_shape=[jax.ShapeDtypeStruct(q.shape, BF16),
                   jax.ShapeDtypeStruct((spans, residues, blk, LANES), F32)],
        compiler_params=_params(len(grid)),
        name=f"attention_d{d}",
    )(q, k, k, v, v, bias)


SSD_SUB = 4


def _head_lane_matrix(n_heads, rows):
    e = np.zeros((rows, n_heads * HEAD_DIM), np.float32)
    for h in range(n_heads):
        e[h, h * HEAD_DIM:(h + 1) * HEAD_DIM] = 1.0
    return jnp.asarray(e, dtype=BF16)


def _group_head_masks():
    m = np.zeros((HEADS_PER_GROUP, SSM_CHUNK, GROUP_WIDTH), np.float32)
    for hh in range(HEADS_PER_GROUP):
        m[hh, :, hh * HEAD_DIM:(hh + 1) * HEAD_DIM] = 1.0
    return jnp.asarray(m, dtype=BF16)


def _ssd_kernel(xc_ref, zs_ref, acs_ref, asrc_ref, wt_ref, spread_ref, mask_ref, dskip_ref, nw_ref,
                y_ref, st_ref):
    @pl.when(pl.program_id(1) == 0)
    def _():
        st_ref[...] = jnp.zeros_like(st_ref)

    for sub in range(SSD_SUB):
        _ssd_chunk(sub, xc_ref, zs_ref, acs_ref, asrc_ref, wt_ref, spread_ref, mask_ref, dskip_ref,
                   nw_ref, y_ref, st_ref)


def _ssd_chunk(sub, xc_ref, zs_ref, acs_ref, asrc_ref, wt_ref, spread_ref, mask_ref, dskip_ref,
               nw_ref, y_ref, st_ref):
    ch = SSM_CHUNK
    rows = slice(sub * ch, (sub + 1) * ch)
    a_cs = acs_ref[0, rows, :]
    a_src_t = asrc_ref[0, sub]
    w_bf = wt_ref[0, sub].astype(BF16)
    row_i = lax.broadcasted_iota(jnp.int32, (ch, ch), 0)
    col_i = lax.broadcasted_iota(jnp.int32, (ch, ch), 1)
    causal = row_i >= col_i
    e = jnp.exp2(a_cs)
    a_exp_all = jnp.dot(e.astype(BF16), spread_ref[...], preferred_element_type=F32)
    e_end = jnp.broadcast_to(e[ch - 1:ch, :], (8, LANES))
    e_end_hi = e_end.astype(BF16)
    e_end_lo = (e_end - e_end_hi.astype(F32)).astype(BF16)
    chunk_decay = (jnp.dot(e_end_hi, spread_ref[...], preferred_element_type=F32)
                   + jnp.dot(e_end_lo, spread_ref[...], preferred_element_type=F32))[0:1]

    for g in range(SSM_GROUPS):
        gs = slice(g * GROUP_WIDTH, (g + 1) * GROUP_WIDTH)
        b_bf = xc_ref[0, rows, SSM_INNER + g * SSM_STATE:SSM_INNER + (g + 1) * SSM_STATE]
        c_bf = xc_ref[0, rows, SSM_INNER + (SSM_GROUPS + g) * SSM_STATE:
                      SSM_INNER + (SSM_GROUPS + g + 1) * SSM_STATE]
        cb = lax.dot_general(c_bf, b_bf, (((1,), (1,)), ((), ())),
                             preferred_element_type=F32).astype(BF16)
        b_t = b_bf.astype(F32).T.astype(BF16)
        x_bf = xc_ref[0, rows, gs]

        intra, to_state, x_diag = [], [], []
        for hh in range(HEADS_PER_GROUP):
            h = g * HEADS_PER_GROUP + hh
            a_col = jnp.broadcast_to(a_cs[:, h:h + 1], (ch, ch))
            decay_dt = jnp.exp2(jnp.where(causal, a_col - a_src_t[h:h + 1, :], NEG))
            intra.append(cb * decay_dt.astype(BF16))
            to_state.append(b_t * w_bf[h:h + 1, :])
            x_diag.append(x_bf * mask_ref[hh])
        lhs = jnp.concatenate([jnp.concatenate(intra, axis=1), jnp.concatenate(to_state, axis=1)], axis=0)
        r = jnp.dot(lhs, jnp.concatenate(x_diag, axis=0), preferred_element_type=F32)
        y_diag, st_inc = r[:ch], r[ch:]

        a_exp = a_exp_all[:, gs]
        st_old = st_ref[g]
        y_off = jnp.dot(c_bf, st_old.astype(BF16), preferred_element_type=F32) * a_exp
        st_ref[g] = st_old * chunk_decay[:, gs] + st_inc

        y = y_diag + y_off + dskip_ref[:, gs] * x_bf.astype(F32)
        y = y * zs_ref[0, rows, gs].astype(F32)
        y_ref[0, rows, gs] = (y * _rms_scale(y) * nw_ref[:, gs]).astype(BF16)


def _ssd(xc, zs, a_cs, a_src_t, w_t, d_skip_wide, norm_w):
    b, s, _ = xc.shape
    ch = SSM_CHUNK * SSD_SUB
    blk = lambda w: pl.BlockSpec((1, ch, w), lambda bi, c: (bi, c, 0))
    per_chunk = pl.BlockSpec((1, SSD_SUB, SSM_HEADS, SSM_CHUNK), lambda bi, c: (bi, c, 0, 0))
    spread = _head_lane_matrix(SSM_HEADS, LANES)
    masks = _group_head_masks()
    return pl.pallas_call(
        _ssd_kernel,
        grid=(b, s // ch),
        in_specs=[blk(CONV_DIM), blk(SSM_INNER), blk(LANES), per_chunk, per_chunk,
                  _const_spec(spread.shape), _const_spec(masks.shape),
                  _const_spec((1, SSM_INNER)), _const_spec((1, SSM_INNER))],
        out_specs=blk(SSM_INNER),
        out_shape=jax.ShapeDtypeStruct((b, s, SSM_INNER), BF16),
        scratch_shapes=[pltpu.VMEM((SSM_GROUPS, SSM_STATE, GROUP_WIDTH), F32)],
        compiler_params=_params(2),
        name="ssd",
    )(xc, zs, a_cs, a_src_t, w_t, spread, masks, d_skip_wide, norm_w)


_ATT_SLABS = ATT_WIDTH // LANES


def _merge_kernel(x_ref, o1_ref, o4_ref, o16_ref, s1_ref, s4_ref, s16_ref, y_ref, gate_ref,
                  expand_ref, watt_ref, wssm_ref, wout_ref, nw_ref, pre_ref, h_ref, f_ref,
                  onat_ref, snat_ref):
    tm = ROW_TILE
    ssm = jnp.dot(y_ref[...], wssm_ref[...], preferred_element_type=F32)
    for slot, (s_ref, o_ref, d) in enumerate(((s1_ref, o1_ref, 4), (s4_ref, o4_ref, 4),
                                              (s16_ref, o16_ref, 16))):
        for r in range(d):
            snat_ref[slot, pl.ds(r, tm // d, stride=d), :] = s_ref[0, r]
        for slab in range(_ATT_SLABS):
            cols = slice(slab * LANES, (slab + 1) * LANES)
            for r in range(d):
                onat_ref[slot, slab, pl.ds(r, tm // d, stride=d), :] = o_ref[0, r, :, cols].astype(F32)

    stats = [snat_ref[0], snat_ref[1], snat_ref[2]]
    top = jnp.maximum(jnp.maximum(stats[0], stats[1]), stats[2])
    scale = [jnp.exp2(st - top) for st in stats]
    dens = [pltpu.roll(st, LANES - DEN_LANE, axis=1) for st in stats]
    total = scale[0] * dens[0] + scale[1] * dens[1] + scale[2] * dens[2]
    is_head = lax.broadcasted_iota(jnp.int32, (tm, LANES), 1) < N_ATT_HEADS
    inv = 1.0 / jnp.where(is_head, total, 1.0)
    wts = [jnp.where(is_head, sc * inv, 0.0).astype(BF16) for sc in scale]
    slabs = []
    for slab in range(_ATT_SLABS):
        cols = slice(slab * LANES, (slab + 1) * LANES)
        wide = [jnp.dot(w, expand_ref[:, cols], preferred_element_type=F32) for w in wts]
        slabs.append(wide[0] * onat_ref[0, slab] + wide[1] * onat_ref[1, slab]
                     + wide[2] * onat_ref[2, slab])
    att = jnp.concatenate(slabs, axis=1).astype(BF16)
    att = jnp.dot(att, watt_ref[...], preferred_element_type=F32)
    gates = gate_ref[...].astype(F32)
    mixed = gates[:, :D_MODEL] * att + gates[:, D_MODEL:] * ssm
    mixed = jnp.dot(mixed.astype(BF16), wout_ref[...], preferred_element_type=F32)
    h = x_ref[...] + mixed * _rms_scale(mixed) * nw_ref[...]
    h_ref[...] = h
    f_ref[...] = (h * _rms_scale(h) * pre_ref[...]).astype(BF16)


def _merge(x2, outs, stats, y, gates, expand, w_att, w_ssm, w_out, norm_w, ffn_pre_w):
    t = x2.shape[0]
    tm = ROW_TILE
    row = lambda width: pl.BlockSpec((tm, width), lambda i: (i, 0))
    return pl.pallas_call(
        _merge_kernel,
        grid=(t // tm,),
        in_specs=[row(D_MODEL),
                  _residue4_spec(ATT_WIDTH), _residue4_spec(ATT_WIDTH), _residue16_spec(ATT_WIDTH),
                  _residue4_spec(LANES), _residue4_spec(LANES), _residue16_spec(LANES),
                  row(SSM_INNER), row(2 * D_MODEL),
                  _const_spec((LANES, ATT_WIDTH)), _const_spec((ATT_WIDTH, D_MODEL)),
                  _const_spec((SSM_INNER, D_MODEL)), _const_spec((D_MODEL, D_MODEL)),
                  _const_spec((1, D_MODEL)), _const_spec((1, D_MODEL))],
        out_specs=[row(D_MODEL), row(D_MODEL)],
        out_shape=[jax.ShapeDtypeStruct((t, D_MODEL), F32), jax.ShapeDtypeStruct((t, D_MODEL), BF16)],
        scratch_shapes=[pltpu.VMEM((3, _ATT_SLABS, tm, LANES), F32),
                        pltpu.VMEM((3, tm, LANES), F32)],
        compiler_params=_params(1),
        name="merge",
    )(x2, *outs, *stats, y, gates, expand, w_att, w_ssm, w_out, norm_w, ffn_pre_w)


FFN_CHUNK = 1024


def _ffn_kernel(h_ref, f_ref, wup_ref, wdown_ref, post_ref, o_ref):
    acc = jnp.zeros((ROW_TILE, D_MODEL), F32)
    for c in range(0, FFN_HIDDEN, FFN_CHUNK):
        up = jnp.dot(f_ref[...], wup_ref[:, c:c + FFN_CHUNK], preferred_element_type=F32)
        act = jnp.square(jnp.maximum(up, 0.0)).astype(BF16)
        acc = acc + jnp.dot(act, wdown_ref[c:c + FFN_CHUNK, :], preferred_element_type=F32)
    o_ref[...] = h_ref[...] + acc * _rms_scale(acc) * post_ref[...]


def _ffn(h2, f2, w_up, w_down, post_w):
    t = h2.shape[0]
    row = pl.BlockSpec((ROW_TILE, D_MODEL), lambda i: (i, 0))
    return pl.pallas_call(
        _ffn_kernel,
        grid=(t // ROW_TILE,),
        in_specs=[row, row, _const_spec((D_MODEL, FFN_HIDDEN)),
                  _const_spec((FFN_HIDDEN, D_MODEL)), _const_spec((1, D_MODEL))],
        out_specs=row,
        out_shape=jax.ShapeDtypeStruct((t, D_MODEL), F32),
        compiler_params=_params(1),
        name="ffn",
    )(h2, f2, w_up, w_down, post_w)


def _pad_lanes(v):
    return jnp.pad(v.astype(F32), (0, LANES - v.shape[0])).reshape(1, LANES)


PREP_ROWS = 256


def _transpose_cast_kernel(wt_ref, o_ref, *, scale):
    o_ref[...] = (wt_ref[...].T * scale).astype(BF16)


def _transpose_cast(wt, row0, n_rows, scale=1.0):
    k = wt.shape[1]
    first = row0 // PREP_ROWS
    assert row0 % PREP_ROWS == 0 and n_rows % PREP_ROWS == 0
    return pl.pallas_call(
        functools.partial(_transpose_cast_kernel, scale=scale),
        grid=(n_rows // PREP_ROWS,),
        in_specs=[pl.BlockSpec((PREP_ROWS, k), lambda j: (first + j, 0))],
        out_specs=pl.BlockSpec((k, PREP_ROWS), lambda j: (0, j)),
        out_shape=jax.ShapeDtypeStruct((k, n_rows), BF16),
        compiler_params=_params(1),
        name="weight_prep",
    )(wt)


def kernel(x, norm_mix_pre_w, w_in, b_gate, conv_w, conv_b, dt_bias, a_log, d_skip, ssm_norm_w,
           w_att_proj, w_ssm_proj, w_out, norm_mix_post_w, norm_ffn_pre_w, w_up, w_down,
           norm_ffn_post_w):
    b, s, d_model = x.shape
    depth = w_in.shape[0]
    assert d_model == D_MODEL and s % SPAN16 == 0
    t = b * s
    expand = _head_lane_matrix(N_ATT_HEADS, LANES)
    h = x.reshape(t, D_MODEL)
    for layer in range(depth):
        w_t_in = jnp.transpose(w_in[layer])
        off_ssm = 3 * ATT_WIDTH
        off_xbc = off_ssm + SSM_INNER
        off_dt = off_xbc + CONV_DIM
        w_qkv = _transpose_cast(w_t_in, 0, off_ssm)
        w_z_half = _transpose_cast(w_t_in, off_ssm, SSM_INNER, 0.5)
        w_xbc = _transpose_cast(w_t_in, off_xbc, CONV_DIM)
        w_gate = jnp.transpose(w_t_in[off_dt + SSM_HEADS:]).astype(BF16)
        w_dt_pad = jnp.pad(jnp.transpose(w_t_in[off_dt:off_dt + SSM_HEADS]),
                           ((0, 0), (0, LANES - SSM_HEADS))).astype(BF16)
        q4, q16, k4, k16, v4, v16, u = _qkv_proj(h, norm_mix_pre_w[layer].reshape(1, -1), w_qkv)
        zs, xc, gates, a_cs, a_src_t, w_t = _ssm_proj(
            u, w_z_half, w_xbc, w_gate, w_dt_pad, b_gate[layer].reshape(1, -1),
            _pad_lanes(dt_bias[layer]),
            0.5 * conv_w[layer], 0.5 * conv_b[layer].reshape(1, -1), _pad_lanes(a_log[layer]), s)
        outs, stats = [], []
        for d in DILATIONS:
            q, k, v = (q16, k16, v16) if d == 16 else (q4, k4, v4)
            o, st = _attention(q, k, v, d, b)
            outs.append(o)
            stats.append(st)
        seq = lambda a: a.reshape(b, s, a.shape[-1])
        per_chunk = lambda a: a.reshape(b, s // SSM_CHUNK, SSM_HEADS, SSM_CHUNK)
        y = _ssd(seq(xc), seq(zs), seq(a_cs), per_chunk(a_src_t), per_chunk(w_t),
                 jnp.repeat(d_skip[layer].astype(F32), HEAD_DIM).reshape(1, -1),
                 ssm_norm_w[layer].reshape(1, -1))
        h, f = _merge(h, outs, stats, y.reshape(t, SSM_INNER), gates, expand,
                      w_att_proj[layer].astype(BF16), w_ssm_proj[layer].astype(BF16),
                      w_out[layer].astype(BF16), norm_mix_post_w[layer].reshape(1, -1),
                      norm_ffn_pre_w[layer].reshape(1, -1))
        h = _ffn(h, f, w_up[layer].astype(BF16), w_down[layer].astype(BF16),
                 norm_ffn_post_w[layer].reshape(1, -1))
    return h.reshape(b, s, D_MODEL)
```

```python
import functools
import math

import numpy as np
import jax
import jax.numpy as jnp
from jax import lax
from jax.experimental import pallas as pl
from jax.experimental.pallas import tpu as pltpu

F32 = jnp.float32
BF16 = jnp.bfloat16

D_MODEL = 1024
HEAD_DIM = 64
N_ATT_HEADS = 12
ATT_WIDTH = N_ATT_HEADS * HEAD_DIM
DILATIONS = (1, 4, 16)
ATT_BLOCK = 128
SSM_INNER = 2048
SSM_HEADS = 32
SSM_GROUPS = 8
SSM_STATE = 128
SSM_CONV = 4
SSM_CHUNK = 128
CONV_DIM = SSM_INNER + 2 * SSM_GROUPS * SSM_STATE
FFN_HIDDEN = 4 * D_MODEL
RMS_EPS = 1e-6
NEG = -1e30

LANES = 128
MXU_N = 256
HEADS_PER_TILE = LANES // HEAD_DIM
N_HEAD_TILES = N_ATT_HEADS // HEADS_PER_TILE
GROUP_WIDTH = SSM_INNER // SSM_GROUPS
HEADS_PER_GROUP = SSM_HEADS // SSM_GROUPS
VMEM_LIMIT_BYTES = 56 * 1024 * 1024

ROW_TILE = 512
SPAN4 = ATT_BLOCK * 4
SPAN16 = ATT_BLOCK * 16
assert ROW_TILE == SPAN4


def _const_spec(shape):
    return pl.BlockSpec(shape, lambda *_: (0,) * len(shape), pipeline_mode=pl.Buffered(1))


def _params(n_axes):
    return pltpu.CompilerParams(dimension_semantics=("arbitrary",) * n_axes,
                                vmem_limit_bytes=VMEM_LIMIT_BYTES)


def _rms_scale(x):
    return lax.rsqrt(jnp.mean(x * x, axis=-1, keepdims=True) + RMS_EPS)


def _residue16_spec(width):
    per_span = SPAN16 // ROW_TILE
    return pl.BlockSpec((1, 16, ROW_TILE // 16, width), lambda i: (i // per_span, 0, i % per_span, 0))


def _residue4_spec(width):
    return pl.BlockSpec((1, 4, ATT_BLOCK, width), lambda i: (i, 0, 0, 0))


_QKV_SLABS = 3 * ATT_WIDTH // LANES
_LOG2E = 1.4426950408889634
Q_SCALE = HEAD_DIM ** -0.5 * _LOG2E


def _qkv_kernel(x_ref, nw_ref, w_ref, *refs):
    out_refs, (u_ref, slab_ref, res4_ref) = refs[:6], refs[6:]
    tm = ROW_TILE
    x = x_ref[...]
    u_ref[...] = (x * _rms_scale(x) * nw_ref[...]).astype(BF16)
    for a in range(3):
        d4_ref, d16_ref = out_refs[2 * a:2 * a + 2]
        for c in range(0, ATT_WIDTH, MXU_N):
            acc = jnp.dot(u_ref[...], w_ref[:, a * ATT_WIDTH + c:a * ATT_WIDTH + c + MXU_N],
                          preferred_element_type=F32)
            if a == 0:
                acc = acc * Q_SCALE
            for half in range(MXU_N // LANES):
                cols = slice(c + half * LANES, c + (half + 1) * LANES)
                slab = (a * ATT_WIDTH + c) // LANES + half
                slab_ref[slab] = acc[:, half * LANES:(half + 1) * LANES]
                for r in range(4):
                    part = slab_ref[slab, pl.ds(r, tm // 4, stride=4), :]
                    d4_ref[0, r, :, cols] = part.astype(BF16)
                    res4_ref[slab, r] = part
                for r in range(16):
                    part = res4_ref[slab, r % 4, pl.ds(r // 4, tm // 16, stride=4), :]
                    d16_ref[0, r, :, cols] = part.astype(BF16)


def _qkv_proj(x2, norm_w, w_qkv):
    t = x2.shape[0]
    tm = ROW_TILE
    shapes = [jax.ShapeDtypeStruct((t // SPAN4, 4, ATT_BLOCK, ATT_WIDTH), BF16),
              jax.ShapeDtypeStruct((t // SPAN16, 16, ATT_BLOCK, ATT_WIDTH), BF16)]
    return pl.pallas_call(
        _qkv_kernel,
        grid=(t // tm,),
        in_specs=[pl.BlockSpec((tm, D_MODEL), lambda i: (i, 0)), _const_spec((1, D_MODEL)),
                  _const_spec((D_MODEL, 3 * ATT_WIDTH))],
        out_specs=[_residue4_spec(ATT_WIDTH), _residue16_spec(ATT_WIDTH)] * 3
                  + [pl.BlockSpec((tm, D_MODEL), lambda i: (i, 0))],
        out_shape=shapes * 3 + [jax.ShapeDtypeStruct((t, D_MODEL), BF16)],
        scratch_shapes=[pltpu.VMEM((_QKV_SLABS, tm, LANES), F32),
                        pltpu.VMEM((_QKV_SLABS, 4, tm // 4, LANES), F32)],
        compiler_params=_params(1),
        name="qkv_proj",
    )(x2, norm_w, w_qkv)


_HALO = 8
_PAD_BUFFERS = 8
PROJ_N = 1024


def _split3(x):
    hi = x.astype(BF16)
    r = x - hi.astype(F32)
    mid = r.astype(BF16)
    lo = (r - mid.astype(F32)).astype(BF16)
    return hi, mid, lo


def _chunk_tri_matrix():
    i = np.arange(ROW_TILE)
    same_chunk = (i[:, None] // SSM_CHUNK) == (i[None, :] // SSM_CHUNK)
    return jnp.asarray(same_chunk & (i[:, None] >= i[None, :]), dtype=BF16)


def _ssm_proj_kernel(u_ref, wz_ref, wx_ref, wg_ref, wdt_ref, bg_ref, dtb_ref, cw_ref, cb_ref,
                     alog_ref, tri_ref, zs_ref, xc_ref, gate_ref, acs_ref, asrc_ref, wt_ref,
                     halo_ref, pad_ref, *, tiles_per_seq):
    tm = ROW_TILE

    @pl.when(pl.program_id(0) % tiles_per_seq == 0)
    def _():
        halo_ref[...] = jnp.zeros_like(halo_ref)

    def proj(w_ref, off, width):
        return jnp.dot(u_ref[...], w_ref[:, off:off + width], preferred_element_type=F32)

    dt_raw = proj(wdt_ref, 0, LANES) + dtb_ref[...]
    dt = jnp.maximum(dt_raw, 0.0) + jnp.log1p(jnp.exp(-jnp.abs(dt_raw)))
    la = dt * (-_LOG2E * jnp.exp(alog_ref[...]))
    a_cs = sum(jnp.dot(tri_ref[...], part, preferred_element_type=F32) for part in _split3(la))
    acs_ref[...] = a_cs
    ch = SSM_CHUNK
    for cidx in range(tm // ch):
        a_t = a_cs[cidx * ch:(cidx + 1) * ch].T
        dt_t = dt[cidx * ch:(cidx + 1) * ch].T
        asrc_ref[cidx] = (a_t - jnp.log2(dt_t))[:SSM_HEADS]
        wt_ref[cidx] = (jnp.exp2(a_t[:, ch - 1:ch] - a_t) * dt_t)[:SSM_HEADS]

    def z_chunk(c):
        h = proj(wz_ref, c, PROJ_N)
        zs_ref[:, c:c + PROJ_N] = (h + h * jnp.tanh(h)).astype(BF16)

    def gate_chunk(c):
        logits = proj(wg_ref, c, PROJ_N) + bg_ref[:, c:c + PROJ_N]
        gate_ref[:, c:c + PROJ_N] = jax.nn.sigmoid(logits).astype(BF16)

    def conv_chunk(idx):
        acc2 = proj(wx_ref, idx * PROJ_N, PROJ_N)
        for part in range(PROJ_N // LANES):
            c = idx * PROJ_N + part * LANES
            cs = slice(c, c + LANES)
            buf = (c // LANES) % _PAD_BUFFERS
            acc = acc2[:, part * LANES:(part + 1) * LANES]
            pad_ref[buf, pl.ds(0, _HALO, stride=2), :] = halo_ref[:, cs]
            pad_ref[buf, pl.ds(2 * _HALO, tm, stride=2), :] = acc
            halo_ref[:, cs] = acc[tm - _HALO:, :]
            h = cb_ref[:, cs] + cw_ref[SSM_CONV - 1:SSM_CONV, cs] * acc
            for tap in range(SSM_CONV - 1):
                start = 2 * (_HALO - (SSM_CONV - 1) + tap)
                h = h + cw_ref[tap:tap + 1, cs] * pad_ref[buf, pl.ds(start, tm, stride=2), :]
            xc_ref[:, cs] = (h + h * jnp.tanh(h)).astype(BF16)

    light = ([functools.partial(z_chunk, c) for c in range(0, SSM_INNER, PROJ_N)]
             + [functools.partial(gate_chunk, c) for c in range(0, 2 * D_MODEL, PROJ_N)])
    for idx in range(CONV_DIM // PROJ_N):
        conv_chunk(idx)
        light[idx]()


def _ssm_proj(u, w_z_half, w_xbc, w_gate, w_dt_pad, b_gate, dt_bias_pad, conv_w_half, conv_b_half,
              a_log_pad, seq_len):
    t = u.shape[0]
    tm = ROW_TILE
    row = lambda width: pl.BlockSpec((tm, width), lambda i: (i, 0))
    per_chunk = pl.BlockSpec((tm // SSM_CHUNK, SSM_HEADS, SSM_CHUNK), lambda i: (i, 0, 0))
    widths = (SSM_INNER, CONV_DIM, 2 * D_MODEL)
    out_shape = [jax.ShapeDtypeStruct((t, w), BF16) for w in widths]
    out_shape.append(jax.ShapeDtypeStruct((t, LANES), F32))
    out_shape += [jax.ShapeDtypeStruct((t // SSM_CHUNK, SSM_HEADS, SSM_CHUNK), F32)] * 2
    tri = _chunk_tri_matrix()
    return pl.pallas_call(
        functools.partial(_ssm_proj_kernel, tiles_per_seq=seq_len // tm),
        grid=(t // tm,),
        in_specs=[row(D_MODEL), _const_spec((D_MODEL, SSM_INNER)), _const_spec((D_MODEL, CONV_DIM)),
                  _const_spec((D_MODEL, 2 * D_MODEL)), _const_spec((D_MODEL, LANES)),
                  _const_spec((1, 2 * D_MODEL)), _const_spec((1, LANES)),
                  _const_spec((SSM_CONV, CONV_DIM)), _const_spec((1, CONV_DIM)),
                  _const_spec((1, LANES)), _const_spec(tri.shape)],
        out_specs=[row(w) for w in widths] + [row(LANES), per_chunk, per_chunk],
        out_shape=out_shape,
        scratch_shapes=[pltpu.VMEM((_HALO, CONV_DIM), F32),
                        pltpu.VMEM((_PAD_BUFFERS, 2 * (_HALO + tm), LANES), F32)],
        compiler_params=_params(1),
        name="ssm_proj",
    )(u, w_z_half, w_xbc, w_gate, w_dt_pad, b_gate, dt_bias_pad, conv_w_half, conv_b_half,
      a_log_pad, tri)


ATT_SUB = 4


def _alibi_slopes(n):
    def pow2(m):
        start = 2.0 ** (-8.0 / m)
        return [start ** (i + 1) for i in range(m)]
    if (n & (n - 1)) == 0:
        s = pow2(n)
    else:
        c = 2 ** int(math.floor(math.log2(n)))
        s = pow2(c) + pow2(2 * c)[0::2][: n - c]
    return np.array(s, dtype=np.float32)


def _attention_bias(dilation, interleave):
    blk = ATT_BLOCK
    seg = blk // interleave
    pos = interleave * (np.arange(blk) % seg) + np.arange(blk) // seg
    i = pos[:, None]
    j = pos[None, :]
    dist_prev = blk + i - j
    dist_cur = i - j
    slopes = (_alibi_slopes(N_ATT_HEADS) * np.float32(_LOG2E)).reshape(N_HEAD_TILES, HEADS_PER_TILE)
    out = np.empty((2, N_HEAD_TILES, HEADS_PER_TILE * blk, 2 * blk), np.float32)
    for tile in range(N_HEAD_TILES):
        for hh in range(HEADS_PER_TILE):
            s = slopes[tile, hh]
            prev = np.where(dist_prev <= blk, -s * (dist_prev * dilation).astype(np.float32), NEG)
            cur = np.where(dist_cur >= 0, -s * (dist_cur * dilation).astype(np.float32), NEG)
            rows = slice(hh * blk, (hh + 1) * blk)
            out[1, tile, rows, :blk] = prev
            out[0, tile, rows, :blk] = NEG
            out[:, tile, rows, blk:] = cur
    return jnp.asarray(out)


DEN_LANE = 16
assert N_ATT_HEADS <= DEN_LANE


def _attention_kernel(q_ref, kc_ref, kp_ref, vc_ref, vp_ref, bias_ref, o_ref, st_ref, *, dilation):
    blk = ATT_BLOCK
    seg = blk // 4
    n = pl.program_id(1)
    lane = lax.broadcasted_iota(jnp.int32, (blk, LANES), 1)
    in_head_a = lane < HEAD_DIM
    keep_a = jnp.where(in_head_a, 1.0, 0.0).astype(BF16)
    keep_b = jnp.where(in_head_a, 0.0, 1.0).astype(BF16)
    ones = jnp.ones((2 * blk, LANES), BF16)

    def block(ref, g, cols):
        if dilation == 1:
            return jnp.concatenate([ref[0, r, g * seg:(g + 1) * seg, cols] for r in range(4)], axis=0)
        return ref[0, g, :, cols]

    def store(ref, g, cols, value):
        if dilation == 1:
            for r in range(4):
                ref[0, r, g * seg:(g + 1) * seg, cols] = value[r * seg:(r + 1) * seg]
        else:
            ref[0, g, :, cols] = value

    for g in range(ATT_SUB):
        stats = jnp.zeros((blk, LANES), F32)
        for tile in range(N_HEAD_TILES):
            cols = slice(tile * LANES, (tile + 1) * LANES)
            if dilation == 1 and g > 0:
                k_prev, v_prev, has_prev = block(kc_ref, g - 1, cols), block(vc_ref, g - 1, cols), None
            elif dilation == 1:
                k_prev = jnp.concatenate([kp_ref[0, r, :, cols] for r in range(4)], axis=0)
                v_prev = jnp.concatenate([vp_ref[0, r, :, cols] for r in range(4)], axis=0)
                has_prev = n > 0
            else:
                k_prev, v_prev, has_prev = kp_ref[0, g, :, cols], vp_ref[0, g, :, cols], n > 0
            q = block(q_ref, g, cols)
            q2 = jnp.concatenate([q * keep_a, q * keep_b], axis=0)
            k2 = jnp.concatenate([k_prev, block(kc_ref, g, cols)], axis=0)
            s = lax.dot_general(q2, k2, (((1,), (1,)), ((), ())), preferred_element_type=F32)
            if has_prev is None:
                s = s + bias_ref[1, tile]
            else:
                s = s + bias_ref[has_prev.astype(jnp.int32), tile]
            m = jnp.max(s, axis=-1, keepdims=True)
            p = jnp.exp2(s - m).astype(BF16)
            v2 = jnp.concatenate([v_prev, block(vc_ref, g, cols)], axis=0)
            r = jnp.dot(p, jnp.concatenate([v2, ones], axis=1), preferred_element_type=F32)
            acc, den = r[:, :LANES], r[:, LANES:]
            store(o_ref, g, cols, jnp.where(in_head_a, acc[:blk], acc[blk:]).astype(BF16))
            for hh in range(HEADS_PER_TILE):
                head = tile * HEADS_PER_TILE + hh
                stats = jnp.where(lane == head, m[hh * blk:(hh + 1) * blk], stats)
                stats = jnp.where(lane == DEN_LANE + head, den[hh * blk:(hh + 1) * blk], stats)
        store(st_ref, g, slice(None), stats)


def _attention(q, k, v, dilation, batch):
    spans, residues, blk, _ = q.shape
    per_batch = spans // batch
    d = dilation
    prev_rows = blk
    if d == 16:
        grid = (batch, per_batch, residues // ATT_SUB)
        cur_map = lambda bi, n, rg: (bi * per_batch + n, rg, 0, 0)
        prev_map = lambda bi, n, rg: (bi * per_batch + jnp.maximum(n - 1, 0), rg, 0, 0)
    else:
        grid = (batch, per_batch)
        cur_map = lambda bi, n: (bi * per_batch + n, 0, 0, 0)
        if d == 1:
            prev_rows = blk // 4
            prev_map = lambda bi, n: (bi * per_batch + jnp.maximum(n - 1, 0), 0, 3, 0)
        else:
            prev_map = lambda bi, n: (bi * per_batch + jnp.maximum(n - 1, 0), 0, 0, 0)
    cur = lambda w: pl.BlockSpec((1, ATT_SUB, blk, w), cur_map)
    prev = lambda w: pl.BlockSpec((1, ATT_SUB, prev_rows, w), prev_map)
    bias = _attention_bias(d, 4 if d == 1 else 1)
    return pl.pallas_call(
        functools.partial(_attention_kernel, dilation=d),
        grid=grid,
        in_specs=[cur(ATT_WIDTH), cur(ATT_WIDTH), prev(ATT_WIDTH), cur(ATT_WIDTH), prev(ATT_WIDTH),
                  _const_spec(bias.shape)],
        out_specs=[cur(ATT_WIDTH), cur(LANES)],
        out_shape=[jax.ShapeDtypeStruct(q.shape, BF16),
                   jax.ShapeDtypeStruct((spans, residues, blk, LANES), F32)],
        compiler_params=_params(len(grid)),
        name=f"attention_d{d}",
    )(q, k, k, v, v, bias)


SSD_SUB = 4


def _head_lane_matrix(n_heads, rows):
    e = np.zeros((rows, n_heads * HEAD_DIM), np.float32)
    for h in range(n_heads):
        e[h, h * HEAD_DIM:(h + 1) * HEAD_DIM] = 1.0
    return jnp.asarray(e, dtype=BF16)


def _group_head_masks():
    m = np.zeros((HEADS_PER_GROUP, SSM_CHUNK, GROUP_WIDTH), np.float32)
    for hh in range(HEADS_PER_GROUP):
        m[hh, :, hh * HEAD_DIM:(hh + 1) * HEAD_DIM] = 1.0
    return jnp.asarray(m, dtype=BF16)


def _ssd_kernel(xc_ref, zs_ref, acs_ref, asrc_ref, wt_ref, spread_ref, mask_ref, dskip_ref, nw_ref,
                y_ref, st_ref):
    @pl.when(pl.program_id(1) == 0)
    def _():
        st_ref[...] = jnp.zeros_like(st_ref)

    for sub in range(SSD_SUB):
        _ssd_chunk(sub, xc_ref, zs_ref, acs_ref, asrc_ref, wt_ref, spread_ref, mask_ref, dskip_ref,
                   nw_ref, y_ref, st_ref)


def _ssd_chunk(sub, xc_ref, zs_ref, acs_ref, asrc_ref, wt_ref, spread_ref, mask_ref, dskip_ref,
               nw_ref, y_ref, st_ref):
    ch = SSM_CHUNK
    rows = slice(sub * ch, (sub + 1) * ch)
    a_cs = acs_ref[0, rows, :]
    a_src_t = asrc_ref[0, sub]
    w_bf = wt_ref[0, sub].astype(BF16)
    row_i = lax.broadcasted_iota(jnp.int32, (ch, ch), 0)
    col_i = lax.broadcasted_iota(jnp.int32, (ch, ch), 1)
    causal = row_i >= col_i
    e = jnp.exp2(a_cs)
    a_exp_all = jnp.dot(e.astype(BF16), spread_ref[...], preferred_element_type=F32)
    e_end = jnp.broadcast_to(e[ch - 1:ch, :], (8, LANES))
    e_end_hi = e_end.astype(BF16)
    e_end_lo = (e_end - e_end_hi.astype(F32)).astype(BF16)
    chunk_decay = (jnp.dot(e_end_hi, spread_ref[...], preferred_element_type=F32)
                   + jnp.dot(e_end_lo, spread_ref[...], preferred_element_type=F32))[0:1]

    for g in range(SSM_GROUPS):
        gs = slice(g * GROUP_WIDTH, (g + 1) * GROUP_WIDTH)
        b_bf = xc_ref[0, rows, SSM_INNER + g * SSM_STATE:SSM_INNER + (g + 1) * SSM_STATE]
        c_bf = xc_ref[0, rows, SSM_INNER + (SSM_GROUPS + g) * SSM_STATE:
                      SSM_INNER + (SSM_GROUPS + g + 1) * SSM_STATE]
        cb = lax.dot_general(c_bf, b_bf, (((1,), (1,)), ((), ())),
                             preferred_element_type=F32).astype(BF16)
        b_t = b_bf.astype(F32).T.astype(BF16)
        x_bf = xc_ref[0, rows, gs]

        intra, to_state, x_diag = [], [], []
        for hh in range(HEADS_PER_GROUP):
            h = g * HEADS_PER_GROUP + hh
            a_col = jnp.broadcast_to(a_cs[:, h:h + 1], (ch, ch))
            decay_dt = jnp.exp2(jnp.where(causal, a_col - a_src_t[h:h + 1, :], NEG))
            intra.append(cb * decay_dt.astype(BF16))
            to_state.append(b_t * w_bf[h:h + 1, :])
            x_diag.append(x_bf * mask_ref[hh])
        lhs = jnp.concatenate([jnp.concatenate(intra, axis=1), jnp.concatenate(to_state, axis=1)], axis=0)
        r = jnp.dot(lhs, jnp.concatenate(x_diag, axis=0), preferred_element_type=F32)
        y_diag, st_inc = r[:ch], r[ch:]

        a_exp = a_exp_all[:, gs]
        st_old = st_ref[g]
        y_off = jnp.dot(c_bf, st_old.astype(BF16), preferred_element_type=F32) * a_exp
        st_ref[g] = st_old * chunk_decay[:, gs] + st_inc

        y = y_diag + y_off + dskip_ref[:, gs] * x_bf.astype(F32)
        y = y * zs_ref[0, rows, gs].astype(F32)
        y_ref[0, rows, gs] = (y * _rms_scale(y) * nw_ref[:, gs]).astype(BF16)


def _ssd(xc, zs, a_cs, a_src_t, w_t, d_skip_wide, norm_w):
    b, s, _ = xc.shape
    ch = SSM_CHUNK * SSD_SUB
    blk = lambda w: pl.BlockSpec((1, ch, w), lambda bi, c: (bi, c, 0))
    per_chunk = pl.BlockSpec((1, SSD_SUB, SSM_HEADS, SSM_CHUNK), lambda bi, c: (bi, c, 0, 0))
    spread = _head_lane_matrix(SSM_HEADS, LANES)
    masks = _group_head_masks()
    return pl.pallas_call(
        _ssd_kernel,
        grid=(b, s // ch),
        in_specs=[blk(CONV_DIM), blk(SSM_INNER), blk(LANES), per_chunk, per_chunk,
                  _const_spec(spread.shape), _const_spec(masks.shape),
                  _const_spec((1, SSM_INNER)), _const_spec((1, SSM_INNER))],
        out_specs=blk(SSM_INNER),
        out_shape=jax.ShapeDtypeStruct((b, s, SSM_INNER), BF16),
        scratch_shapes=[pltpu.VMEM((SSM_GROUPS, SSM_STATE, GROUP_WIDTH), F32)],
        compiler_params=_params(2),
        name="ssd",
    )(xc, zs, a_cs, a_src_t, w_t, spread, masks, d_skip_wide, norm_w)


_ATT_SLABS = ATT_WIDTH // LANES


def _merge_kernel(x_ref, o1_ref, o4_ref, o16_ref, s1_ref, s4_ref, s16_ref, y_ref, gate_ref,
                  expand_ref, watt_ref, wssm_ref, wout_ref, nw_ref, pre_ref, h_ref, f_ref,
                  onat_ref, snat_ref):
    tm = ROW_TILE
    ssm = jnp.dot(y_ref[...], wssm_ref[...], preferred_element_type=F32)
    for slot, (s_ref, o_ref, d) in enumerate(((s1_ref, o1_ref, 4), (s4_ref, o4_ref, 4),
                                              (s16_ref, o16_ref, 16))):
        for r in range(d):
            snat_ref[slot, pl.ds(r, tm // d, stride=d), :] = s_ref[0, r]
        for slab in range(_ATT_SLABS):
            cols = slice(slab * LANES, (slab + 1) * LANES)
            for r in range(d):
                onat_ref[slot, slab, pl.ds(r, tm // d, stride=d), :] = o_ref[0, r, :, cols].astype(F32)

    stats = [snat_ref[0], snat_ref[1], snat_ref[2]]
    top = jnp.maximum(jnp.maximum(stats[0], stats[1]), stats[2])
    scale = [jnp.exp2(st - top) for st in stats]
    dens = [pltpu.roll(st, LANES - DEN_LANE, axis=1) for st in stats]
    total = scale[0] * dens[0] + scale[1] * dens[1] + scale[2] * dens[2]
    is_head = lax.broadcasted_iota(jnp.int32, (tm, LANES), 1) < N_ATT_HEADS
    inv = 1.0 / jnp.where(is_head, total, 1.0)
    wts = [jnp.where(is_head, sc * inv, 0.0).astype(BF16) for sc in scale]
    slabs = []
    for slab in range(_ATT_SLABS):
        cols = slice(slab * LANES, (slab + 1) * LANES)
        wide = [jnp.dot(w, expand_ref[:, cols], preferred_element_type=F32) for w in wts]
        slabs.append(wide[0] * onat_ref[0, slab] + wide[1] * onat_ref[1, slab]
                     + wide[2] * onat_ref[2, slab])
    att = jnp.concatenate(slabs, axis=1).astype(BF16)
    att = jnp.dot(att, watt_ref[...], preferred_element_type=F32)
    gates = gate_ref[...].astype(F32)
    mixed = gates[:, :D_MODEL] * att + gates[:, D_MODEL:] * ssm
    mixed = jnp.dot(mixed.astype(BF16), wout_ref[...], preferred_element_type=F32)
    h = x_ref[...] + mixed * _rms_scale(mixed) * nw_ref[...]
    h_ref[...] = h
    f_ref[...] = (h * _rms_scale(h) * pre_ref[...]).astype(BF16)


def _merge(x2, outs, stats, y, gates, expand, w_att, w_ssm, w_out, norm_w, ffn_pre_w):
    t = x2.shape[0]
    tm = ROW_TILE
    row = lambda width: pl.BlockSpec((tm, width), lambda i: (i, 0))
    return pl.pallas_call(
        _merge_kernel,
        grid=(t // tm,),
        in_specs=[row(D_MODEL),
                  _residue4_spec(ATT_WIDTH), _residue4_spec(ATT_WIDTH), _residue16_spec(ATT_WIDTH),
                  _residue4_spec(LANES), _residue4_spec(LANES), _residue16_spec(LANES),
                  row(SSM_INNER), row(2 * D_MODEL),
                  _const_spec((LANES, ATT_WIDTH)), _const_spec((ATT_WIDTH, D_MODEL)),
                  _const_spec((SSM_INNER, D_MODEL)), _const_spec((D_MODEL, D_MODEL)),
                  _const_spec((1, D_MODEL)), _const_spec((1, D_MODEL))],
        out_specs=[row(D_MODEL), row(D_MODEL)],
        out_shape=[jax.ShapeDtypeStruct((t, D_MODEL), F32), jax.ShapeDtypeStruct((t, D_MODEL), BF16)],
        scratch_shapes=[pltpu.VMEM((3, _ATT_SLABS, tm, LANES), F32),
                        pltpu.VMEM((3, tm, LANES), F32)],
        compiler_params=_params(1),
        name="merge",
    )(x2, *outs, *stats, y, gates, expand, w_att, w_ssm, w_out, norm_w, ffn_pre_w)


FFN_CHUNK = 1024


def _ffn_kernel(h_ref, f_ref, wup_ref, wdown_ref, post_ref, o_ref):
    acc = jnp.zeros((ROW_TILE, D_MODEL), F32)
    for c in range(0, FFN_HIDDEN, FFN_CHUNK):
        up = jnp.dot(f_ref[...], wup_ref[:, c:c + FFN_CHUNK], preferred_element_type=F32)
        act = jnp.square(jnp.maximum(up, 0.0)).astype(BF16)
        acc = acc + jnp.dot(act, wdown_ref[c:c + FFN_CHUNK, :], preferred_element_type=F32)
    o_ref[...] = h_ref[...] + acc * _rms_scale(acc) * post_ref[...]


def _ffn(h2, f2, w_up, w_down, post_w):
    t = h2.shape[0]
    row = pl.BlockSpec((ROW_TILE, D_MODEL), lambda i: (i, 0))
    return pl.pallas_call(
        _ffn_kernel,
        grid=(t // ROW_TILE,),
        in_specs=[row, row, _const_spec((D_MODEL, FFN_HIDDEN)),
                  _const_spec((FFN_HIDDEN, D_MODEL)), _const_spec((1, D_MODEL))],
        out_specs=row,
        out_shape=jax.ShapeDtypeStruct((t, D_MODEL), F32),
        compiler_params=_params(1),
        name="ffn",
    )(h2, f2, w_up, w_down, post_w)


def _pad_lanes(v):
    return jnp.pad(v.astype(F32), (0, LANES - v.shape[0])).reshape(1, LANES)


PREP_ROWS = 256


def _transpose_cast_kernel(wt_ref, o_ref, *, scale):
    o_ref[...] = (wt_ref[...].T * scale).astype(BF16)


def _transpose_cast(wt, row0, n_rows, scale=1.0):
    k = wt.shape[1]
    first = row0 // PREP_ROWS
    assert row0 % PREP_ROWS == 0 and n_rows % PREP_ROWS == 0
    return pl.pallas_call(
        functools.partial(_transpose_cast_kernel, scale=scale),
        grid=(n_rows // PREP_ROWS,),
        in_specs=[pl.BlockSpec((PREP_ROWS, k), lambda j: (first + j, 0))],
        out_specs=pl.BlockSpec((k, PREP_ROWS), lambda j: (0, j)),
        out_shape=jax.ShapeDtypeStruct((k, n_rows), BF16),
        compiler_params=_params(1),
        name="weight_prep",
    )(wt)


def kernel(x, norm_mix_pre_w, w_in, b_gate, conv_w, conv_b, dt_bias, a_log, d_skip, ssm_norm_w,
           w_att_proj, w_ssm_proj, w_out, norm_mix_post_w, norm_ffn_pre_w, w_up, w_down,
           norm_ffn_post_w):
    b, s, d_model = x.shape
    depth = w_in.shape[0]
    assert d_model == D_MODEL and s % SPAN16 == 0
    t = b * s
    expand = _head_lane_matrix(N_ATT_HEADS, LANES)
    h = x.reshape(t, D_MODEL)
    for layer in range(depth):
        w_t_in = jnp.transpose(w_in[layer])
        off_ssm = 3 * ATT_WIDTH
        off_xbc = off_ssm + SSM_INNER
        off_dt = off_xbc + CONV_DIM
        w_qkv = _transpose_cast(w_t_in, 0, off_ssm)
        w_z_half = _transpose_cast(w_t_in, off_ssm, SSM_INNER, 0.5)
        w_xbc = _transpose_cast(w_t_in, off_xbc, CONV_DIM)
        n_tail = 2 * D_MODEL + SSM_HEADS
        w_tail = _transpose_cast(w_t_in, off_dt, -(-n_tail // PREP_ROWS) * PREP_ROWS)
        w_gate = w_tail[:, SSM_HEADS:n_tail]
        w_dt_pad = jnp.where(jnp.arange(LANES) < SSM_HEADS, w_tail[:, :LANES], 0).astype(BF16)
        q4, q16, k4, k16, v4, v16, u = _qkv_proj(h, norm_mix_pre_w[layer].reshape(1, -1), w_qkv)
        zs, xc, gates, a_cs, a_src_t, w_t = _ssm_proj(
            u, w_z_half, w_xbc, w_gate, w_dt_pad, b_gate[layer].reshape(1, -1),
            _pad_lanes(dt_bias[layer]),
            0.5 * conv_w[layer], 0.5 * conv_b[layer].reshape(1, -1), _pad_lanes(a_log[layer]), s)
        outs, stats = [], []
        for d in DILATIONS:
            q, k, v = (q16, k16, v16) if d == 16 else (q4, k4, v4)
            o, st = _attention(q, k, v, d, b)
            outs.append(o)
            stats.append(st)
        seq = lambda a: a.reshape(b, s, a.shape[-1])
        per_chunk = lambda a: a.reshape(b, s // SSM_CHUNK, SSM_HEADS, SSM_CHUNK)
        y = _ssd(seq(xc), seq(zs), seq(a_cs), per_chunk(a_src_t), per_chunk(w_t),
                 jnp.repeat(d_skip[layer].astype(F32), HEAD_DIM).reshape(1, -1),
                 ssm_norm_w[layer].reshape(1, -1))
        h, f = _merge(h, outs, stats, y.reshape(t, SSM_INNER), gates, expand,
                      w_att_proj[layer].astype(BF16), w_ssm_proj[layer].astype(BF16),
                      w_out[layer].astype(BF16), norm_mix_post_w[layer].reshape(1, -1),
                      norm_ffn_pre_w[layer].reshape(1, -1))
        h = _ffn(h, f, w_up[layer].astype(BF16), w_down[layer].astype(BF16),
                 norm_ffn_post_w[layer].reshape(1, -1))
    return h.reshape(b, s, D_MODEL)
```

```python
import functools
import math

import numpy as np
import jax
import jax.numpy as jnp
from jax import lax
from jax.experimental import pallas as pl
from jax.experimental.pallas import tpu as pltpu

F32 = jnp.float32
BF16 = jnp.bfloat16

D_MODEL = 1024
HEAD_DIM = 64
N_ATT_HEADS = 12
ATT_WIDTH = N_ATT_HEADS * HEAD_DIM
DILATIONS = (1, 4, 16)
ATT_BLOCK = 128
SSM_INNER = 2048
SSM_HEADS = 32
SSM_GROUPS = 8
SSM_STATE = 128
SSM_CONV = 4
SSM_CHUNK = 128
CONV_DIM = SSM_INNER + 2 * SSM_GROUPS * SSM_STATE
FFN_HIDDEN = 4 * D_MODEL
RMS_EPS = 1e-6
NEG = -1e30

LANES = 128
MXU_N = 256
HEADS_PER_TILE = LANES // HEAD_DIM
N_HEAD_TILES = N_ATT_HEADS // HEADS_PER_TILE
GROUP_WIDTH = SSM_INNER // SSM_GROUPS
HEADS_PER_GROUP = SSM_HEADS // SSM_GROUPS
VMEM_LIMIT_BYTES = 56 * 1024 * 1024

ROW_TILE = 512
SPAN4 = ATT_BLOCK * 4
SPAN16 = ATT_BLOCK * 16
assert ROW_TILE == SPAN4


def _const_spec(shape):
    return pl.BlockSpec(shape, lambda *_: (0,) * len(shape), pipeline_mode=pl.Buffered(1))


def _params(n_axes):
    return pltpu.CompilerParams(dimension_semantics=("arbitrary",) * n_axes,
                                vmem_limit_bytes=VMEM_LIMIT_BYTES)


def _rms_scale(x):
    return lax.rsqrt(jnp.mean(x * x, axis=-1, keepdims=True) + RMS_EPS)


def _residue16_spec(width):
    per_span = SPAN16 // ROW_TILE
    return pl.BlockSpec((1, 16, ROW_TILE // 16, width), lambda i: (i // per_span, 0, i % per_span, 0))


def _residue4_spec(width):
    return pl.BlockSpec((1, 4, ATT_BLOCK, width), lambda i: (i, 0, 0, 0))


_QKV_SLABS = 3 * ATT_WIDTH // LANES
_LOG2E = 1.4426950408889634
Q_SCALE = HEAD_DIM ** -0.5 * _LOG2E


def _qkv_kernel(x_ref, nw_ref, w_ref, *refs):
    d4_ref, d16_ref, u_ref, slab_ref, res4_ref = refs
    tm = ROW_TILE
    x = x_ref[...]
    u_ref[...] = (x * _rms_scale(x) * nw_ref[...]).astype(BF16)
    for a in range(3):
        for c in range(0, ATT_WIDTH, MXU_N):
            acc = jnp.dot(u_ref[...], w_ref[:, a * ATT_WIDTH + c:a * ATT_WIDTH + c + MXU_N],
                          preferred_element_type=F32)
            if a == 0:
                acc = acc * Q_SCALE
            for half in range(MXU_N // LANES):
                slab = (a * ATT_WIDTH + c) // LANES + half
                lane0 = ((a + 2) % 3) * ATT_WIDTH + c + half * LANES
                cols = slice(lane0, lane0 + LANES)
                slab_ref[slab] = acc[:, half * LANES:(half + 1) * LANES]
                for r in range(4):
                    part = slab_ref[slab, pl.ds(r, tm // 4, stride=4), :]
                    d4_ref[0, r, :, cols] = part.astype(BF16)
                    res4_ref[slab, r] = part
                for r in range(16):
                    part = res4_ref[slab, r % 4, pl.ds(r // 4, tm // 16, stride=4), :]
                    d16_ref[0, r, :, cols] = part.astype(BF16)


def _qkv_proj(x2, norm_w, w_qkv):
    t = x2.shape[0]
    tm = ROW_TILE
    shapes = [jax.ShapeDtypeStruct((t // SPAN4, 4, ATT_BLOCK, 3 * ATT_WIDTH), BF16),
              jax.ShapeDtypeStruct((t // SPAN16, 16, ATT_BLOCK, 3 * ATT_WIDTH), BF16)]
    return pl.pallas_call(
        _qkv_kernel,
        grid=(t // tm,),
        in_specs=[pl.BlockSpec((tm, D_MODEL), lambda i: (i, 0)), _const_spec((1, D_MODEL)),
                  _const_spec((D_MODEL, 3 * ATT_WIDTH))],
        out_specs=[_residue4_spec(3 * ATT_WIDTH), _residue16_spec(3 * ATT_WIDTH),
                   pl.BlockSpec((tm, D_MODEL), lambda i: (i, 0))],
        out_shape=shapes + [jax.ShapeDtypeStruct((t, D_MODEL), BF16)],
        scratch_shapes=[pltpu.VMEM((_QKV_SLABS, tm, LANES), F32),
                        pltpu.VMEM((_QKV_SLABS, 4, tm // 4, LANES), F32)],
        compiler_params=_params(1),
        name="qkv_proj",
    )(x2, norm_w, w_qkv)


_HALO = 8
_PAD_BUFFERS = 8
PROJ_N = 1024


def _split3(x):
    hi = x.astype(BF16)
    r = x - hi.astype(F32)
    mid = r.astype(BF16)
    lo = (r - mid.astype(F32)).astype(BF16)
    return hi, mid, lo


def _chunk_tri_matrix():
    i = np.arange(ROW_TILE)
    same_chunk = (i[:, None] // SSM_CHUNK) == (i[None, :] // SSM_CHUNK)
    return jnp.asarray(same_chunk & (i[:, None] >= i[None, :]), dtype=BF16)


def _ssm_proj_kernel(u_ref, wz_ref, wx_ref, wg_ref, wdt_ref, bg_ref, dtb_ref, cw_ref, cb_ref,
                     alog_ref, tri_ref, zs_ref, xc_ref, gate_ref, acs_ref, asrc_ref, wt_ref,
                     halo_ref, pad_ref, *, tiles_per_seq):
    tm = ROW_TILE

    @pl.when(pl.program_id(0) % tiles_per_seq == 0)
    def _():
        halo_ref[...] = jnp.zeros_like(halo_ref)

    def proj(w_ref, off, width):
        return jnp.dot(u_ref[...], w_ref[:, off:off + width], preferred_element_type=F32)

    dt_raw = proj(wdt_ref, 0, LANES) + dtb_ref[...]
    dt = jnp.maximum(dt_raw, 0.0) + jnp.log1p(jnp.exp(-jnp.abs(dt_raw)))
    la = dt * (-_LOG2E * jnp.exp(alog_ref[...]))
    a_cs = sum(jnp.dot(tri_ref[...], part, preferred_element_type=F32) for part in _split3(la))
    acs_ref[...] = a_cs
    ch = SSM_CHUNK
    for cidx in range(tm // ch):
        a_t = a_cs[cidx * ch:(cidx + 1) * ch].T
        dt_t = dt[cidx * ch:(cidx + 1) * ch].T
        asrc_ref[cidx] = (a_t - jnp.log2(dt_t))[:SSM_HEADS]
        wt_ref[cidx] = (jnp.exp2(a_t[:, ch - 1:ch] - a_t) * dt_t)[:SSM_HEADS]

    def z_chunk(c):
        h = proj(wz_ref, c, PROJ_N)
        zs_ref[:, c:c + PROJ_N] = (h + h * jnp.tanh(h)).astype(BF16)

    def gate_chunk(c):
        h = proj(wg_ref, c, PROJ_N) + bg_ref[:, c:c + PROJ_N]
        gate_ref[:, c:c + PROJ_N] = (0.5 + 0.5 * jnp.tanh(h)).astype(BF16)

    def conv_chunk(idx):
        acc2 = proj(wx_ref, idx * PROJ_N, PROJ_N)
        for part in range(PROJ_N // LANES):
            c = idx * PROJ_N + part * LANES
            cs = slice(c, c + LANES)
            buf = (c // LANES) % _PAD_BUFFERS
            acc = acc2[:, part * LANES:(part + 1) * LANES]
            pad_ref[buf, pl.ds(0, _HALO, stride=2), :] = halo_ref[:, cs]
            pad_ref[buf, pl.ds(2 * _HALO, tm, stride=2), :] = acc
            halo_ref[:, cs] = acc[tm - _HALO:, :]
            h = cb_ref[:, cs] + cw_ref[SSM_CONV - 1:SSM_CONV, cs] * acc
            for tap in range(SSM_CONV - 1):
                start = 2 * (_HALO - (SSM_CONV - 1) + tap)
                h = h + cw_ref[tap:tap + 1, cs] * pad_ref[buf, pl.ds(start, tm, stride=2), :]
            xc_ref[:, cs] = (h + h * jnp.tanh(h)).astype(BF16)

    light = ([functools.partial(z_chunk, c) for c in range(0, SSM_INNER, PROJ_N)]
             + [functools.partial(gate_chunk, c) for c in range(0, 2 * D_MODEL, PROJ_N)])
    for idx in range(CONV_DIM // PROJ_N):
        conv_chunk(idx)
        light[idx]()


def _ssm_proj(u, w_z_half, w_xbc, w_gate, w_dt_pad, b_gate, dt_bias_pad, conv_w_half, conv_b_half,
              a_log_pad, seq_len):
    t = u.shape[0]
    tm = ROW_TILE
    row = lambda width: pl.BlockSpec((tm, width), lambda i: (i, 0))
    per_chunk = pl.BlockSpec((tm // SSM_CHUNK, SSM_HEADS, SSM_CHUNK), lambda i: (i, 0, 0))
    widths = (SSM_INNER, CONV_DIM, 2 * D_MODEL)
    out_shape = [jax.ShapeDtypeStruct((t, w), BF16) for w in widths]
    out_shape.append(jax.ShapeDtypeStruct((t, LANES), F32))
    out_shape += [jax.ShapeDtypeStruct((t // SSM_CHUNK, SSM_HEADS, SSM_CHUNK), F32)] * 2
    tri = _chunk_tri_matrix()
    return pl.pallas_call(
        functools.partial(_ssm_proj_kernel, tiles_per_seq=seq_len // tm),
        grid=(t // tm,),
        in_specs=[row(D_MODEL), _const_spec((D_MODEL, SSM_INNER)), _const_spec((D_MODEL, CONV_DIM)),
                  _const_spec((D_MODEL, 2 * D_MODEL)), _const_spec((D_MODEL, LANES)),
                  _const_spec((1, 2 * D_MODEL)), _const_spec((1, LANES)),
                  _const_spec((SSM_CONV, CONV_DIM)), _const_spec((1, CONV_DIM)),
                  _const_spec((1, LANES)), _const_spec(tri.shape)],
        out_specs=[row(w) for w in widths] + [row(LANES), per_chunk, per_chunk],
        out_shape=out_shape,
        scratch_shapes=[pltpu.VMEM((_HALO, CONV_DIM), F32),
                        pltpu.VMEM((_PAD_BUFFERS, 2 * (_HALO + tm), LANES), F32)],
        compiler_params=_params(1),
        name="ssm_proj",
    )(u, w_z_half, w_xbc, w_gate, w_dt_pad, b_gate, dt_bias_pad, conv_w_half, conv_b_half,
      a_log_pad, tri)


ATT_SUB = 16
ATT_SUB16 = 16


def _alibi_slopes(n):
    def pow2(m):
        start = 2.0 ** (-8.0 / m)
        return [start ** (i + 1) for i in range(m)]
    if (n & (n - 1)) == 0:
        s = pow2(n)
    else:
        c = 2 ** int(math.floor(math.log2(n)))
        s = pow2(c) + pow2(2 * c)[0::2][: n - c]
    return np.array(s, dtype=np.float32)


def _attention_bias(dilation, interleave):
    blk = ATT_BLOCK
    seg = blk // interleave
    pos = interleave * (np.arange(blk) % seg) + np.arange(blk) // seg
    i = pos[:, None]
    j = pos[None, :]
    dist_prev = blk + i - j
    dist_cur = i - j
    slopes = (_alibi_slopes(N_ATT_HEADS) * np.float32(_LOG2E)).reshape(N_HEAD_TILES, HEADS_PER_TILE)
    out = np.empty((2, N_HEAD_TILES, HEADS_PER_TILE * blk, 2 * blk), np.float32)
    for tile in range(N_HEAD_TILES):
        for hh in range(HEADS_PER_TILE):
            s = slopes[tile, hh]
            prev = np.where(dist_prev <= blk, -s * (dist_prev * dilation).astype(np.float32), NEG)
            cur = np.where(dist_cur >= 0, -s * (dist_cur * dilation).astype(np.float32), NEG)
            rows = slice(hh * blk, (hh + 1) * blk)
            out[1, tile, rows, :blk] = prev
            out[0, tile, rows, :blk] = NEG
            out[:, tile, rows, blk:] = cur
    return jnp.asarray(out)


DEN_LANE = 16
assert N_ATT_HEADS <= DEN_LANE


def _attention_kernel(cur_ref, prev_ref, bias_ref, o_ref, st_ref, *, dilation):
    blk = ATT_BLOCK
    seg = blk // 4
    n_spans, n_res = cur_ref.shape[0], cur_ref.shape[1]
    n = pl.program_id(1)
    lane = lax.broadcasted_iota(jnp.int32, (blk, LANES), 1)
    in_head_a = lane < HEAD_DIM
    keep_a = jnp.where(in_head_a, 1.0, 0.0).astype(BF16)
    keep_b = jnp.where(in_head_a, 0.0, 1.0).astype(BF16)
    ones = jnp.ones((2 * blk, LANES), BF16)

    def block(ref, sp, g, cols):
        if dilation == 1:
            return jnp.concatenate([ref[sp, r, g * seg:(g + 1) * seg, cols] for r in range(4)], axis=0)
        return ref[sp, g, :, cols]

    def store(ref, sp, g, cols, value):
        if dilation == 1:
            for r in range(4):
                ref[sp, r, g * seg:(g + 1) * seg, cols] = value[r * seg:(r + 1) * seg]
        else:
            ref[sp, g, :, cols] = value

    def previous(sp, g, cols):
        if dilation == 1:
            if g > 0:
                return block(cur_ref, sp, g - 1, cols), None
            if sp > 0:
                return block(cur_ref, sp - 1, n_res - 1, cols), None
            return jnp.concatenate([prev_ref[0, r, :, cols] for r in range(4)], axis=0), n > 0
        if sp > 0:
            return cur_ref[sp - 1, g, :, cols], None
        return prev_ref[0, g, :, cols], n > 0

    for sp, g in [(sp, g) for sp in range(n_spans) for g in range(n_res)]:
        stats = jnp.zeros((blk, LANES), F32)
        for tile in range(N_HEAD_TILES):
            cols = slice(tile * LANES, (tile + 1) * LANES)
            k_cols = cols
            v_cols = slice(ATT_WIDTH + tile * LANES, ATT_WIDTH + (tile + 1) * LANES)
            q_cols = slice(2 * ATT_WIDTH + tile * LANES, 2 * ATT_WIDTH + (tile + 1) * LANES)
            k_prev, has_prev = previous(sp, g, k_cols)
            v_prev, _ = previous(sp, g, v_cols)
            q = block(cur_ref, sp, g, q_cols)
            q2 = jnp.concatenate([q * keep_a, q * keep_b], axis=0)
            k2 = jnp.concatenate([k_prev, block(cur_ref, sp, g, k_cols)], axis=0)
            s = lax.dot_general(q2, k2, (((1,), (1,)), ((), ())), preferred_element_type=F32)
            if has_prev is None:
                s = s + bias_ref[1, tile]
            else:
                s = s + bias_ref[has_prev.astype(jnp.int32), tile]
            m = jnp.max(s, axis=-1, keepdims=True)
            p = jnp.exp2(s - m).astype(BF16)
            v2 = jnp.concatenate([v_prev, block(cur_ref, sp, g, v_cols)], axis=0)
            r = jnp.dot(p, jnp.concatenate([v2, ones], axis=1), preferred_element_type=F32)
            acc, den = r[:, :LANES], r[:, LANES:]
            store(o_ref, sp, g, cols, jnp.where(in_head_a, acc[:blk], acc[blk:]).astype(BF16))
            for hh in range(HEADS_PER_TILE):
                head = tile * HEADS_PER_TILE + hh
                stats = jnp.where(lane == head, m[hh * blk:(hh + 1) * blk], stats)
                stats = jnp.where(lane == DEN_LANE + head, den[hh * blk:(hh + 1) * blk], stats)
        store(st_ref, sp, g, slice(None), stats)


def _attention(qkv, dilation, batch):
    spans, residues, blk, _ = qkv.shape
    per_batch = spans // batch
    d = dilation
    prev_rows = blk
    if d == 16:
        step_spans, step_res = 1, ATT_SUB16
        grid = (batch, per_batch, residues // step_res)
        cur_map = lambda bi, n, rg: (bi * per_batch + n, rg, 0, 0)
        prev_map = lambda bi, n, rg: (bi * per_batch + jnp.maximum(n - 1, 0), rg, 0, 0)
    else:
        step_spans, step_res = ATT_SUB // residues, residues
        steps = per_batch // step_spans
        grid = (batch, steps)
        cur_map = lambda bi, n: (bi * steps + n, 0, 0, 0)
        last_span = lambda bi, n: bi * per_batch + jnp.maximum(step_spans * n - 1, 0)
        if d == 1:
            prev_rows = blk // 4
            prev_map = lambda bi, n: (last_span(bi, n), 0, 3, 0)
        else:
            prev_map = lambda bi, n: (last_span(bi, n), 0, 0, 0)
    cur = lambda w: pl.BlockSpec((step_spans, step_res, blk, w), cur_map)
    prev = lambda w: pl.BlockSpec((1, step_res, prev_rows, w), prev_map)
    bias = _attention_bias(d, 4 if d == 1 else 1)
    return pl.pallas_call(
        functools.partial(_attention_kernel, dilation=d),
        grid=grid,
        in_specs=[cur(3 * ATT_WIDTH), prev(2 * ATT_WIDTH), _const_spec(bias.shape)],
        out_specs=[cur(ATT_WIDTH), cur(LANES)],
        out_shape=[jax.ShapeDtypeStruct((spans, residues, blk, ATT_WIDTH), BF16),
                   jax.ShapeDtypeStruct((spans, residues, blk, LANES), F32)],
        compiler_params=_params(len(grid)),
        name=f"attention_d{d}",
    )(qkv, qkv, bias)


SSD_SUB = 8


def _head_lane_matrix(n_heads, rows):
    e = np.zeros((rows, n_heads * HEAD_DIM), np.float32)
    for h in range(n_heads):
        e[h, h * HEAD_DIM:(h + 1) * HEAD_DIM] = 1.0
    return jnp.asarray(e, dtype=BF16)


def _group_head_masks():
    m = np.zeros((HEADS_PER_GROUP, SSM_CHUNK, GROUP_WIDTH), np.float32)
    for hh in range(HEADS_PER_GROUP):
        m[hh, :, hh * HEAD_DIM:(hh + 1) * HEAD_DIM] = 1.0
    return jnp.asarray(m, dtype=BF16)


def _ssd_kernel(xc_ref, zs_ref, acs_ref, asrc_ref, wt_ref, spread_ref, mask_ref, dskip_ref, nw_ref,
                y_ref, st_ref):
    @pl.when(pl.program_id(1) == 0)
    def _():
        st_ref[...] = jnp.zeros_like(st_ref)

    for sub in range(SSD_SUB):
        _ssd_chunk(sub, xc_ref, zs_ref, acs_ref, asrc_ref, wt_ref, spread_ref, mask_ref, dskip_ref,
                   nw_ref, y_ref, st_ref)


def _ssd_chunk(sub, xc_ref, zs_ref, acs_ref, asrc_ref, wt_ref, spread_ref, mask_ref, dskip_ref,
               nw_ref, y_ref, st_ref):
    ch = SSM_CHUNK
    rows = slice(sub * ch, (sub + 1) * ch)
    a_cs = acs_ref[0, rows, :]
    a_src_t = asrc_ref[0, sub]
    w_bf = wt_ref[0, sub].astype(BF16)
    row_i = lax.broadcasted_iota(jnp.int32, (ch, ch), 0)
    col_i = lax.broadcasted_iota(jnp.int32, (ch, ch), 1)
    causal = row_i >= col_i
    e = jnp.exp2(a_cs)
    a_exp_all = jnp.dot(e.astype(BF16), spread_ref[...], preferred_element_type=F32)
    e_end = jnp.broadcast_to(e[ch - 1:ch, :], (8, LANES))
    e_end_hi = e_end.astype(BF16)
    e_end_lo = (e_end - e_end_hi.astype(F32)).astype(BF16)
    chunk_decay = (jnp.dot(e_end_hi, spread_ref[...], preferred_element_type=F32)
                   + jnp.dot(e_end_lo, spread_ref[...], preferred_element_type=F32))[0:1]

    for g in range(SSM_GROUPS):
        gs = slice(g * GROUP_WIDTH, (g + 1) * GROUP_WIDTH)
        b_bf = xc_ref[0, rows, SSM_INNER + g * SSM_STATE:SSM_INNER + (g + 1) * SSM_STATE]
        c_bf = xc_ref[0, rows, SSM_INNER + (SSM_GROUPS + g) * SSM_STATE:
                      SSM_INNER + (SSM_GROUPS + g + 1) * SSM_STATE]
        cb = lax.dot_general(c_bf, b_bf, (((1,), (1,)), ((), ())),
                             preferred_element_type=F32).astype(BF16)
        b_t = b_bf.astype(F32).T.astype(BF16)
        x_bf = xc_ref[0, rows, gs]

        intra, to_state, x_diag = [], [], []
        for hh in range(HEADS_PER_GROUP):
            h = g * HEADS_PER_GROUP + hh
            a_col = jnp.broadcast_to(a_cs[:, h:h + 1], (ch, ch))
            decay_dt = jnp.exp2(jnp.where(causal, a_col - a_src_t[h:h + 1, :], NEG))
            intra.append(cb * decay_dt.astype(BF16))
            to_state.append(b_t * w_bf[h:h + 1, :])
            x_diag.append(x_bf * mask_ref[hh])
        lhs = jnp.concatenate([jnp.concatenate(intra, axis=1), jnp.concatenate(to_state, axis=1)], axis=0)
        r = jnp.dot(lhs, jnp.concatenate(x_diag, axis=0), preferred_element_type=F32)
        y_diag, st_inc = r[:ch], r[ch:]

        a_exp = a_exp_all[:, gs]
        st_old = st_ref[g]
        y_off = jnp.dot(c_bf, st_old.astype(BF16), preferred_element_type=F32) * a_exp
        st_ref[g] = st_old * chunk_decay[:, gs] + st_inc

        y = y_diag + y_off + dskip_ref[:, gs] * x_bf.astype(F32)
        y = y * zs_ref[0, rows, gs].astype(F32)
        y_ref[0, rows, gs] = (y * _rms_scale(y) * nw_ref[:, gs]).astype(BF16)


def _ssd(xc, zs, a_cs, a_src_t, w_t, d_skip_wide, norm_w):
    b, s, _ = xc.shape
    ch = SSM_CHUNK * SSD_SUB
    blk = lambda w: pl.BlockSpec((1, ch, w), lambda bi, c: (bi, c, 0))
    per_chunk = pl.BlockSpec((1, SSD_SUB, SSM_HEADS, SSM_CHUNK), lambda bi, c: (bi, c, 0, 0))
    spread = _head_lane_matrix(SSM_HEADS, LANES)
    masks = _group_head_masks()
    return pl.pallas_call(
        _ssd_kernel,
        grid=(b, s // ch),
        in_specs=[blk(CONV_DIM), blk(SSM_INNER), blk(LANES), per_chunk, per_chunk,
                  _const_spec(spread.shape), _const_spec(masks.shape),
                  _const_spec((1, SSM_INNER)), _const_spec((1, SSM_INNER))],
        out_specs=blk(SSM_INNER),
        out_shape=jax.ShapeDtypeStruct((b, s, SSM_INNER), BF16),
        scratch_shapes=[pltpu.VMEM((SSM_GROUPS, SSM_STATE, GROUP_WIDTH), F32)],
        compiler_params=_params(2),
        name="ssd",
    )(xc, zs, a_cs, a_src_t, w_t, spread, masks, d_skip_wide, norm_w)


_ATT_SLABS = ATT_WIDTH // LANES


def _merge_kernel(x_ref, o1_ref, o4_ref, o16_ref, s1_ref, s4_ref, s16_ref, y_ref, gate_ref,
                  expand_ref, watt_ref, wssm_ref, wout_ref, nw_ref, pre_ref, h_ref, f_ref,
                  onat_ref, snat_ref):
    tm = ROW_TILE
    ssm = jnp.dot(y_ref[...], wssm_ref[...], preferred_element_type=F32)
    for slot, (s_ref, o_ref, d) in enumerate(((s1_ref, o1_ref, 4), (s4_ref, o4_ref, 4),
                                              (s16_ref, o16_ref, 16))):
        for r in range(d):
            snat_ref[slot, pl.ds(r, tm // d, stride=d), :] = s_ref[0, r]
        for slab in range(_ATT_SLABS):
            cols = slice(slab * LANES, (slab + 1) * LANES)
            for r in range(d):
                onat_ref[slot, slab, pl.ds(r, tm // d, stride=d), :] = o_ref[0, r, :, cols].astype(F32)

    stats = [snat_ref[0], snat_ref[1], snat_ref[2]]
    top = jnp.maximum(jnp.maximum(stats[0], stats[1]), stats[2])
    scale = [jnp.exp2(st - top) for st in stats]
    dens = [pltpu.roll(st, LANES - DEN_LANE, axis=1) for st in stats]
    total = scale[0] * dens[0] + scale[1] * dens[1] + scale[2] * dens[2]
    is_head = lax.broadcasted_iota(jnp.int32, (tm, LANES), 1) < N_ATT_HEADS
    inv = 1.0 / jnp.where(is_head, total, 1.0)
    wts = [jnp.where(is_head, sc * inv, 0.0).astype(BF16) for sc in scale]
    slabs = []
    for slab in range(_ATT_SLABS):
        cols = slice(slab * LANES, (slab + 1) * LANES)
        wide = [jnp.dot(w, expand_ref[:, cols], preferred_element_type=F32) for w in wts]
        slabs.append(wide[0] * onat_ref[0, slab] + wide[1] * onat_ref[1, slab]
                     + wide[2] * onat_ref[2, slab])
    att = jnp.concatenate(slabs, axis=1).astype(BF16)
    att = jnp.dot(att, watt_ref[...], preferred_element_type=F32)
    gates = gate_ref[...].astype(F32)
    mixed = gates[:, :D_MODEL] * att + gates[:, D_MODEL:] * ssm
    mixed = jnp.dot(mixed.astype(BF16), wout_ref[...], preferred_element_type=F32)
    h = x_ref[...] + mixed * _rms_scale(mixed) * nw_ref[...]
    h_ref[...] = h
    f_ref[...] = (h * _rms_scale(h) * pre_ref[...]).astype(BF16)


def _merge(x2, outs, stats, y, gates, expand, w_att, w_ssm, w_out, norm_w, ffn_pre_w):
    t = x2.shape[0]
    tm = ROW_TILE
    row = lambda width: pl.BlockSpec((tm, width), lambda i: (i, 0))
    return pl.pallas_call(
        _merge_kernel,
        grid=(t // tm,),
        in_specs=[row(D_MODEL),
                  _residue4_spec(ATT_WIDTH), _residue4_spec(ATT_WIDTH), _residue16_spec(ATT_WIDTH),
                  _residue4_spec(LANES), _residue4_spec(LANES), _residue16_spec(LANES),
                  row(SSM_INNER), row(2 * D_MODEL),
                  _const_spec((LANES, ATT_WIDTH)), _const_spec((ATT_WIDTH, D_MODEL)),
                  _const_spec((SSM_INNER, D_MODEL)), _const_spec((D_MODEL, D_MODEL)),
                  _const_spec((1, D_MODEL)), _const_spec((1, D_MODEL))],
        out_specs=[row(D_MODEL), row(D_MODEL)],
        out_shape=[jax.ShapeDtypeStruct((t, D_MODEL), F32), jax.ShapeDtypeStruct((t, D_MODEL), BF16)],
        scratch_shapes=[pltpu.VMEM((3, _ATT_SLABS, tm, LANES), F32),
                        pltpu.VMEM((3, tm, LANES), F32)],
        compiler_params=_params(1),
        name="merge",
    )(x2, *outs, *stats, y, gates, expand, w_att, w_ssm, w_out, norm_w, ffn_pre_w)


FFN_CHUNK = 1024
FFN_TILE = 1024


def _ffn_kernel(h_ref, f_ref, wup_ref, wdown_ref, post_ref, o_ref):
    acc = jnp.zeros((FFN_TILE, D_MODEL), F32)
    for c in range(0, FFN_HIDDEN, FFN_CHUNK):
        up = jnp.dot(f_ref[...], wup_ref[:, c:c + FFN_CHUNK], preferred_element_type=F32)
        act = jnp.square(jnp.maximum(up, 0.0)).astype(BF16)
        acc = acc + jnp.dot(act, wdown_ref[c:c + FFN_CHUNK, :], preferred_element_type=F32)
    o_ref[...] = h_ref[...] + acc * _rms_scale(acc) * post_ref[...]


def _ffn(h2, f2, w_up, w_down, post_w):
    t = h2.shape[0]
    row = pl.BlockSpec((FFN_TILE, D_MODEL), lambda i: (i, 0))
    return pl.pallas_call(
        _ffn_kernel,
        grid=(t // FFN_TILE,),
        in_specs=[row, row, _const_spec((D_MODEL, FFN_HIDDEN)),
                  _const_spec((FFN_HIDDEN, D_MODEL)), _const_spec((1, D_MODEL))],
        out_specs=row,
        out_shape=jax.ShapeDtypeStruct((t, D_MODEL), F32),
        compiler_params=_params(1),
        name="ffn",
    )(h2, f2, w_up, w_down, post_w)


def _pad_lanes(v):
    return jnp.pad(v.astype(F32), (0, LANES - v.shape[0])).reshape(1, LANES)


PREP_ROWS = 256


def _transpose_cast_kernel(wt_ref, o_ref, *, scale):
    o_ref[...] = (wt_ref[...].T * scale).astype(BF16)


def _transpose_cast(wt, row0, n_rows, scale=1.0):
    k = wt.shape[1]
    first = row0 // PREP_ROWS
    assert row0 % PREP_ROWS == 0 and n_rows % PREP_ROWS == 0
    return pl.pallas_call(
        functools.partial(_transpose_cast_kernel, scale=scale),
        grid=(n_rows // PREP_ROWS,),
        in_specs=[pl.BlockSpec((PREP_ROWS, k), lambda j: (first + j, 0))],
        out_specs=pl.BlockSpec((k, PREP_ROWS), lambda j: (0, j)),
        out_shape=jax.ShapeDtypeStruct((k, n_rows), BF16),
        compiler_params=_params(1),
        name="weight_prep",
    )(wt)


def kernel(x, norm_mix_pre_w, w_in, b_gate, conv_w, conv_b, dt_bias, a_log, d_skip, ssm_norm_w,
           w_att_proj, w_ssm_proj, w_out, norm_mix_post_w, norm_ffn_pre_w, w_up, w_down,
           norm_ffn_post_w):
    b, s, d_model = x.shape
    depth = w_in.shape[0]
    assert d_model == D_MODEL and s % SPAN16 == 0
    t = b * s
    expand = _head_lane_matrix(N_ATT_HEADS, LANES)
    h = x.reshape(t, D_MODEL)
    for layer in range(depth):
        w_t_in = jnp.transpose(w_in[layer])
        off_ssm = 3 * ATT_WIDTH
        off_xbc = off_ssm + SSM_INNER
        off_dt = off_xbc + CONV_DIM
        w_qkv = _transpose_cast(w_t_in, 0, off_ssm)
        w_z_half = _transpose_cast(w_t_in, off_ssm, SSM_INNER, 0.5)
        w_xbc = _transpose_cast(w_t_in, off_xbc, CONV_DIM)
        n_tail = 2 * D_MODEL + SSM_HEADS
        w_tail = _transpose_cast(w_t_in, off_dt, -(-n_tail // PREP_ROWS) * PREP_ROWS)
        w_gate_half = 0.5 * w_tail[:, SSM_HEADS:n_tail]
        w_dt_pad = jnp.where(jnp.arange(LANES) < SSM_HEADS, w_tail[:, :LANES], 0).astype(BF16)
        qkv4, qkv16, u = _qkv_proj(h, norm_mix_pre_w[layer].reshape(1, -1), w_qkv)
        zs, xc, gates, a_cs, a_src_t, w_t = _ssm_proj(
            u, w_z_half, w_xbc, w_gate_half, w_dt_pad, 0.5 * b_gate[layer].reshape(1, -1),
            _pad_lanes(dt_bias[layer]),
            0.5 * conv_w[layer], 0.5 * conv_b[layer].reshape(1, -1), _pad_lanes(a_log[layer]), s)
        outs, stats = [], []
        for d in DILATIONS:
            o, st = _attention(qkv16 if d == 16 else qkv4, d, b)
            outs.append(o)
            stats.append(st)
        seq = lambda a: a.reshape(b, s, a.shape[-1])
        per_chunk = lambda a: a.reshape(b, s // SSM_CHUNK, SSM_HEADS, SSM_CHUNK)
        y = _ssd(seq(xc), seq(zs), seq(a_cs), per_chunk(a_src_t), per_chunk(w_t),
                 jnp.repeat(d_skip[layer].astype(F32), HEAD_DIM).reshape(1, -1),
                 ssm_norm_w[layer].reshape(1, -1))
        h, f = _merge(h, outs, stats, y.reshape(t, SSM_INNER), gates, expand,
                      w_att_proj[layer].astype(BF16), w_ssm_proj[layer].astype(BF16),
                      w_out[layer].astype(BF16), norm_mix_post_w[layer].reshape(1, -1),
                      norm_ffn_pre_w[layer].reshape(1, -1))
        h = _ffn(h, f, w_up[layer].astype(BF16), w_down[layer].astype(BF16),
                 norm_ffn_post_w[layer].reshape(1, -1))
    return h.reshape(b, s, D_MODEL)
```

```python
import functools
import math

import numpy as np
import jax
import jax.numpy as jnp
from jax import lax
from jax.experimental import pallas as pl
from jax.experimental.pallas import tpu as pltpu

F32 = jnp.float32
BF16 = jnp.bfloat16

D_MODEL = 1024
HEAD_DIM = 64
N_ATT_HEADS = 12
ATT_WIDTH = N_ATT_HEADS * HEAD_DIM
DILATIONS = (1, 4, 16)
ATT_BLOCK = 128
SSM_INNER = 2048
SSM_HEADS = 32
SSM_GROUPS = 8
SSM_STATE = 128
SSM_CONV = 4
SSM_CHUNK = 128
CONV_DIM = SSM_INNER + 2 * SSM_GROUPS * SSM_STATE
FFN_HIDDEN = 4 * D_MODEL
RMS_EPS = 1e-6
NEG = -1e30

LANES = 128
MXU_N = 256
HEADS_PER_TILE = LANES // HEAD_DIM
N_HEAD_TILES = N_ATT_HEADS // HEADS_PER_TILE
GROUP_WIDTH = SSM_INNER // SSM_GROUPS
HEADS_PER_GROUP = SSM_HEADS // SSM_GROUPS
VMEM_LIMIT_BYTES = 56 * 1024 * 1024

ROW_TILE = 512
SPAN4 = ATT_BLOCK * 4
SPAN16 = ATT_BLOCK * 16
assert ROW_TILE == SPAN4


def _const_spec(shape):
    return pl.BlockSpec(shape, lambda *_: (0,) * len(shape), pipeline_mode=pl.Buffered(1))


def _params(n_axes):
    return pltpu.CompilerParams(dimension_semantics=("arbitrary",) * n_axes,
                                vmem_limit_bytes=VMEM_LIMIT_BYTES)


def _rms_scale(x):
    return lax.rsqrt(jnp.mean(x * x, axis=-1, keepdims=True) + RMS_EPS)


def _residue16_spec(width):
    per_span = SPAN16 // ROW_TILE
    return pl.BlockSpec((1, 16, ROW_TILE // 16, width), lambda i: (i // per_span, 0, i % per_span, 0))


def _residue4_spec(width):
    return pl.BlockSpec((1, 4, ATT_BLOCK, width), lambda i: (i, 0, 0, 0))


_QKV_SLABS = 3 * ATT_WIDTH // LANES
_LOG2E = 1.4426950408889634
Q_SCALE = HEAD_DIM ** -0.5 * _LOG2E


def _qkv_kernel(x_ref, nw_ref, w_ref, *refs):
    d4_ref, d16_ref, u_ref, slab_ref, res4_ref = refs
    tm = ROW_TILE
    x = x_ref[...]
    u_ref[...] = (x * _rms_scale(x) * nw_ref[...]).astype(BF16)
    for a in range(3):
        for c in range(0, ATT_WIDTH, MXU_N):
            acc = jnp.dot(u_ref[...], w_ref[:, a * ATT_WIDTH + c:a * ATT_WIDTH + c + MXU_N],
                          preferred_element_type=F32)
            if a == 0:
                acc = acc * Q_SCALE
            for half in range(MXU_N // LANES):
                slab = (a * ATT_WIDTH + c) // LANES + half
                lane0 = ((a + 2) % 3) * ATT_WIDTH + c + half * LANES
                cols = slice(lane0, lane0 + LANES)
                slab_ref[slab] = acc[:, half * LANES:(half + 1) * LANES]
                for r in range(4):
                    part = slab_ref[slab, pl.ds(r, tm // 4, stride=4), :]
                    d4_ref[0, r, :, cols] = part.astype(BF16)
                    res4_ref[slab, r] = part
                for r in range(16):
                    part = res4_ref[slab, r % 4, pl.ds(r // 4, tm // 16, stride=4), :]
                    d16_ref[0, r, :, cols] = part.astype(BF16)


def _qkv_proj(x2, norm_w, w_qkv):
    t = x2.shape[0]
    tm = ROW_TILE
    shapes = [jax.ShapeDtypeStruct((t // SPAN4, 4, ATT_BLOCK, 3 * ATT_WIDTH), BF16),
              jax.ShapeDtypeStruct((t // SPAN16, 16, ATT_BLOCK, 3 * ATT_WIDTH), BF16)]
    return pl.pallas_call(
        _qkv_kernel,
        grid=(t // tm,),
        in_specs=[pl.BlockSpec((tm, D_MODEL), lambda i: (i, 0)), _const_spec((1, D_MODEL)),
                  _const_spec((D_MODEL, 3 * ATT_WIDTH))],
        out_specs=[_residue4_spec(3 * ATT_WIDTH), _residue16_spec(3 * ATT_WIDTH),
                   pl.BlockSpec((tm, D_MODEL), lambda i: (i, 0))],
        out_shape=shapes + [jax.ShapeDtypeStruct((t, D_MODEL), BF16)],
        scratch_shapes=[pltpu.VMEM((_QKV_SLABS, tm, LANES), F32),
                        pltpu.VMEM((_QKV_SLABS, 4, tm // 4, LANES), F32)],
        compiler_params=_params(1),
        name="qkv_proj",
    )(x2, norm_w, w_qkv)


_HALO = 8
_PAD_BUFFERS = 8
PROJ_N = 1024


def _split3(x):
    hi = x.astype(BF16)
    r = x - hi.astype(F32)
    mid = r.astype(BF16)
    lo = (r - mid.astype(F32)).astype(BF16)
    return hi, mid, lo


def _chunk_tri_matrix():
    i = np.arange(ROW_TILE)
    same_chunk = (i[:, None] // SSM_CHUNK) == (i[None, :] // SSM_CHUNK)
    return jnp.asarray(same_chunk & (i[:, None] >= i[None, :]), dtype=BF16)


def _ssm_proj_kernel(u_ref, wz_ref, wx_ref, wg_ref, wdt_ref, bg_ref, dtb_ref, cw_ref, cb_ref,
                     alog_ref, tri_ref, zs_ref, xc_ref, gate_ref, acs_ref, asrc_ref, wt_ref,
                     halo_ref, pad_ref, *, tiles_per_seq):
    tm = ROW_TILE

    @pl.when(pl.program_id(0) % tiles_per_seq == 0)
    def _():
        halo_ref[...] = jnp.zeros_like(halo_ref)

    def proj(w_ref, off, width):
        return jnp.dot(u_ref[...], w_ref[:, off:off + width], preferred_element_type=F32)

    dt_raw = proj(wdt_ref, 0, LANES) + dtb_ref[...]
    dt = jnp.maximum(dt_raw, 0.0) + jnp.log1p(jnp.exp(-jnp.abs(dt_raw)))
    la = dt * (-_LOG2E * jnp.exp(alog_ref[...]))
    a_cs = sum(jnp.dot(tri_ref[...], part, preferred_element_type=F32) for part in _split3(la))
    acs_ref[...] = a_cs
    ch = SSM_CHUNK
    for cidx in range(tm // ch):
        a_t = a_cs[cidx * ch:(cidx + 1) * ch].T
        dt_t = dt[cidx * ch:(cidx + 1) * ch].T
        asrc_ref[cidx] = (a_t - jnp.log2(dt_t))[:SSM_HEADS]
        wt_ref[cidx] = (jnp.exp2(a_t[:, ch - 1:ch] - a_t) * dt_t)[:SSM_HEADS]

    def z_chunk(c):
        h = proj(wz_ref, c, PROJ_N)
        zs_ref[:, c:c + PROJ_N] = (h + h * jnp.tanh(h)).astype(BF16)

    def gate_chunk(c):
        h = proj(wg_ref, c, PROJ_N) + bg_ref[:, c:c + PROJ_N]
        gate_ref[:, c:c + PROJ_N] = (0.5 + 0.5 * jnp.tanh(h)).astype(BF16)

    def conv_chunk(idx):
        acc2 = proj(wx_ref, idx * PROJ_N, PROJ_N)
        for part in range(PROJ_N // LANES):
            c = idx * PROJ_N + part * LANES
            cs = slice(c, c + LANES)
            buf = (c // LANES) % _PAD_BUFFERS
            acc = acc2[:, part * LANES:(part + 1) * LANES]
            pad_ref[buf, pl.ds(0, _HALO, stride=2), :] = halo_ref[:, cs]
            pad_ref[buf, pl.ds(2 * _HALO, tm, stride=2), :] = acc
            halo_ref[:, cs] = acc[tm - _HALO:, :]
            h = cb_ref[:, cs] + cw_ref[SSM_CONV - 1:SSM_CONV, cs] * acc
            for tap in range(SSM_CONV - 1):
                start = 2 * (_HALO - (SSM_CONV - 1) + tap)
                h = h + cw_ref[tap:tap + 1, cs] * pad_ref[buf, pl.ds(start, tm, stride=2), :]
            xc_ref[:, cs] = (h + h * jnp.tanh(h)).astype(BF16)

    light = ([functools.partial(z_chunk, c) for c in range(0, SSM_INNER, PROJ_N)]
             + [functools.partial(gate_chunk, c) for c in range(0, 2 * D_MODEL, PROJ_N)])
    for idx in range(CONV_DIM // PROJ_N):
        conv_chunk(idx)
        light[idx]()


def _ssm_proj(u, w_z_half, w_xbc, w_gate, w_dt_pad, b_gate, dt_bias_pad, conv_w_half, conv_b_half,
              a_log_pad, seq_len):
    t = u.shape[0]
    tm = ROW_TILE
    row = lambda width: pl.BlockSpec((tm, width), lambda i: (i, 0))
    per_chunk = pl.BlockSpec((tm // SSM_CHUNK, SSM_HEADS, SSM_CHUNK), lambda i: (i, 0, 0))
    widths = (SSM_INNER, CONV_DIM, 2 * D_MODEL)
    out_shape = [jax.ShapeDtypeStruct((t, w), BF16) for w in widths]
    out_shape.append(jax.ShapeDtypeStruct((t, LANES), F32))
    out_shape += [jax.ShapeDtypeStruct((t // SSM_CHUNK, SSM_HEADS, SSM_CHUNK), F32)] * 2
    tri = _chunk_tri_matrix()
    return pl.pallas_call(
        functools.partial(_ssm_proj_kernel, tiles_per_seq=seq_len // tm),
        grid=(t // tm,),
        in_specs=[row(D_MODEL), _const_spec((D_MODEL, SSM_INNER)), _const_spec((D_MODEL, CONV_DIM)),
                  _const_spec((D_MODEL, 2 * D_MODEL)), _const_spec((D_MODEL, LANES)),
                  _const_spec((1, 2 * D_MODEL)), _const_spec((1, LANES)),
                  _const_spec((SSM_CONV, CONV_DIM)), _const_spec((1, CONV_DIM)),
                  _const_spec((1, LANES)), _const_spec(tri.shape)],
        out_specs=[row(w) for w in widths] + [row(LANES), per_chunk, per_chunk],
        out_shape=out_shape,
        scratch_shapes=[pltpu.VMEM((_HALO, CONV_DIM), F32),
                        pltpu.VMEM((_PAD_BUFFERS, 2 * (_HALO + tm), LANES), F32)],
        compiler_params=_params(1),
        name="ssm_proj",
    )(u, w_z_half, w_xbc, w_gate, w_dt_pad, b_gate, dt_bias_pad, conv_w_half, conv_b_half,
      a_log_pad, tri)


ATT_SUB = 16
ATT_SUB16 = 16


def _alibi_slopes(n):
    def pow2(m):
        start = 2.0 ** (-8.0 / m)
        return [start ** (i + 1) for i in range(m)]
    if (n & (n - 1)) == 0:
        s = pow2(n)
    else:
        c = 2 ** int(math.floor(math.log2(n)))
        s = pow2(c) + pow2(2 * c)[0::2][: n - c]
    return np.array(s, dtype=np.float32)


def _attention_bias(dilation, interleave):
    blk = ATT_BLOCK
    seg = blk // interleave
    pos = interleave * (np.arange(blk) % seg) + np.arange(blk) // seg
    i = pos[:, None]
    j = pos[None, :]
    dist_prev = blk + i - j
    dist_cur = i - j
    slopes = (_alibi_slopes(N_ATT_HEADS) * np.float32(_LOG2E)).reshape(N_HEAD_TILES, HEADS_PER_TILE)
    out = np.empty((2, N_HEAD_TILES, HEADS_PER_TILE * blk, 2 * blk), np.float32)
    for tile in range(N_HEAD_TILES):
        for hh in range(HEADS_PER_TILE):
            s = slopes[tile, hh]
            prev = np.where(dist_prev <= blk, -s * (dist_prev * dilation).astype(np.float32), NEG)
            cur = np.where(dist_cur >= 0, -s * (dist_cur * dilation).astype(np.float32), NEG)
            rows = slice(hh * blk, (hh + 1) * blk)
            out[1, tile, rows, :blk] = prev
            out[0, tile, rows, :blk] = NEG
            out[:, tile, rows, blk:] = cur
    return jnp.asarray(out)


DEN_LANE = 16
assert N_ATT_HEADS <= DEN_LANE


def _attention_kernel(cur_ref, prev_ref, bias_ref, o_ref, st_ref, *, dilation):
    blk = ATT_BLOCK
    seg = blk // 4
    n_spans, n_res = cur_ref.shape[0], cur_ref.shape[1]
    n = pl.program_id(1)
    lane = lax.broadcasted_iota(jnp.int32, (blk, LANES), 1)
    in_head_a = lane < HEAD_DIM
    keep_a = jnp.where(in_head_a, 1.0, 0.0).astype(BF16)
    keep_b = jnp.where(in_head_a, 0.0, 1.0).astype(BF16)
    ones = jnp.ones((2 * blk, LANES), BF16)

    def block(ref, sp, g, cols):
        if dilation == 1:
            return jnp.concatenate([ref[sp, r, g * seg:(g + 1) * seg, cols] for r in range(4)], axis=0)
        return ref[sp, g, :, cols]

    def store(ref, sp, g, cols, value):
        if dilation == 1:
            for r in range(4):
                ref[sp, r, g * seg:(g + 1) * seg, cols] = value[r * seg:(r + 1) * seg]
        else:
            ref[sp, g, :, cols] = value

    def previous(sp, g, cols):
        if dilation == 1:
            if g > 0:
                return block(cur_ref, sp, g - 1, cols), None
            if sp > 0:
                return block(cur_ref, sp - 1, n_res - 1, cols), None
            return jnp.concatenate([prev_ref[0, r, :, cols] for r in range(4)], axis=0), n > 0
        if sp > 0:
            return cur_ref[sp - 1, g, :, cols], None
        return prev_ref[0, g, :, cols], n > 0

    for sp, g in [(sp, g) for sp in range(n_spans) for g in range(n_res)]:
        stats = jnp.zeros((blk, LANES), F32)
        for tile in range(N_HEAD_TILES):
            cols = slice(tile * LANES, (tile + 1) * LANES)
            k_cols = cols
            v_cols = slice(ATT_WIDTH + tile * LANES, ATT_WIDTH + (tile + 1) * LANES)
            q_cols = slice(2 * ATT_WIDTH + tile * LANES, 2 * ATT_WIDTH + (tile + 1) * LANES)
            k_prev, has_prev = previous(sp, g, k_cols)
            v_prev, _ = previous(sp, g, v_cols)
            q = block(cur_ref, sp, g, q_cols)
            q2 = jnp.concatenate([q * keep_a, q * keep_b], axis=0)
            k2 = jnp.concatenate([k_prev, block(cur_ref, sp, g, k_cols)], axis=0)
            s = lax.dot_general(q2, k2, (((1,), (1,)), ((), ())), preferred_element_type=F32)
            if has_prev is None:
                s = s + bias_ref[1, tile]
            else:
                s = s + bias_ref[has_prev.astype(jnp.int32), tile]
            m = jnp.max(s, axis=-1, keepdims=True)
            p = jnp.exp2(s - m).astype(BF16)
            v2 = jnp.concatenate([v_prev, block(cur_ref, sp, g, v_cols)], axis=0)
            r = jnp.dot(p, jnp.concatenate([v2, ones], axis=1), preferred_element_type=F32)
            acc, den = r[:, :LANES], r[:, LANES:]
            store(o_ref, sp, g, cols, jnp.where(in_head_a, acc[:blk], acc[blk:]).astype(BF16))
            for hh in range(HEADS_PER_TILE):
                head = tile * HEADS_PER_TILE + hh
                stats = jnp.where(lane == head, m[hh * blk:(hh + 1) * blk], stats)
                stats = jnp.where(lane == DEN_LANE + head, den[hh * blk:(hh + 1) * blk], stats)
        store(st_ref, sp, g, slice(None), stats)


def _attention(qkv, dilation, batch):
    spans, residues, blk, _ = qkv.shape
    per_batch = spans // batch
    d = dilation
    prev_rows = blk
    if d == 16:
        step_spans, step_res = 1, ATT_SUB16
        grid = (batch, per_batch, residues // step_res)
        cur_map = lambda bi, n, rg: (bi * per_batch + n, rg, 0, 0)
        prev_map = lambda bi, n, rg: (bi * per_batch + jnp.maximum(n - 1, 0), rg, 0, 0)
    else:
        step_spans, step_res = ATT_SUB // residues, residues
        steps = per_batch // step_spans
        grid = (batch, steps)
        cur_map = lambda bi, n: (bi * steps + n, 0, 0, 0)
        last_span = lambda bi, n: bi * per_batch + jnp.maximum(step_spans * n - 1, 0)
        if d == 1:
            prev_rows = blk // 4
            prev_map = lambda bi, n: (last_span(bi, n), 0, 3, 0)
        else:
            prev_map = lambda bi, n: (last_span(bi, n), 0, 0, 0)
    cur = lambda w: pl.BlockSpec((step_spans, step_res, blk, w), cur_map)
    prev = lambda w: pl.BlockSpec((1, step_res, prev_rows, w), prev_map)
    bias = _attention_bias(d, 4 if d == 1 else 1)
    return pl.pallas_call(
        functools.partial(_attention_kernel, dilation=d),
        grid=grid,
        in_specs=[cur(3 * ATT_WIDTH), prev(2 * ATT_WIDTH), _const_spec(bias.shape)],
        out_specs=[cur(ATT_WIDTH), cur(LANES)],
        out_shape=[jax.ShapeDtypeStruct((spans, residues, blk, ATT_WIDTH), BF16),
                   jax.ShapeDtypeStruct((spans, residues, blk, LANES), F32)],
        compiler_params=_params(len(grid)),
        name=f"attention_d{d}",
    )(qkv, qkv, bias)


SSD_SUB = 8


def _head_lane_matrix(n_heads, rows):
    e = np.zeros((rows, n_heads * HEAD_DIM), np.float32)
    for h in range(n_heads):
        e[h, h * HEAD_DIM:(h + 1) * HEAD_DIM] = 1.0
    return jnp.asarray(e, dtype=BF16)


def _group_head_masks():
    m = np.zeros((HEADS_PER_GROUP, SSM_CHUNK, GROUP_WIDTH), np.float32)
    for hh in range(HEADS_PER_GROUP):
        m[hh, :, hh * HEAD_DIM:(hh + 1) * HEAD_DIM] = 1.0
    return jnp.asarray(m, dtype=BF16)


def _ssd_kernel(xc_ref, zs_ref, acs_ref, asrc_ref, wt_ref, spread_ref, mask_ref, dskip_ref, nw_ref,
                y_ref, st_ref):
    @pl.when(pl.program_id(1) == 0)
    def _():
        st_ref[...] = jnp.zeros_like(st_ref)

    for sub in range(SSD_SUB):
        _ssd_chunk(sub, xc_ref, zs_ref, acs_ref, asrc_ref, wt_ref, spread_ref, mask_ref, dskip_ref,
                   nw_ref, y_ref, st_ref)


def _ssd_chunk(sub, xc_ref, zs_ref, acs_ref, asrc_ref, wt_ref, spread_ref, mask_ref, dskip_ref,
               nw_ref, y_ref, st_ref):
    ch = SSM_CHUNK
    rows = slice(sub * ch, (sub + 1) * ch)
    a_cs = acs_ref[0, rows, :]
    a_src_t = asrc_ref[0, sub]
    w_bf = wt_ref[0, sub].astype(BF16)
    row_i = lax.broadcasted_iota(jnp.int32, (ch, ch), 0)
    col_i = lax.broadcasted_iota(jnp.int32, (ch, ch), 1)
    causal = row_i >= col_i
    e = jnp.exp2(a_cs)
    a_exp_all = jnp.dot(e.astype(BF16), spread_ref[...], preferred_element_type=F32)
    e_end = jnp.broadcast_to(e[ch - 1:ch, :], (8, LANES))
    e_end_hi = e_end.astype(BF16)
    e_end_lo = (e_end - e_end_hi.astype(F32)).astype(BF16)
    chunk_decay = (jnp.dot(e_end_hi, spread_ref[...], preferred_element_type=F32)
                   + jnp.dot(e_end_lo, spread_ref[...], preferred_element_type=F32))[0:1]

    for g in range(SSM_GROUPS):
        gs = slice(g * GROUP_WIDTH, (g + 1) * GROUP_WIDTH)
        b_bf = xc_ref[0, rows, SSM_INNER + g * SSM_STATE:SSM_INNER + (g + 1) * SSM_STATE]
        c_bf = xc_ref[0, rows, SSM_INNER + (SSM_GROUPS + g) * SSM_STATE:
                      SSM_INNER + (SSM_GROUPS + g + 1) * SSM_STATE]
        cb = lax.dot_general(c_bf, b_bf, (((1,), (1,)), ((), ())),
                             preferred_element_type=F32).astype(BF16)
        b_t = b_bf.astype(F32).T.astype(BF16)
        x_bf = xc_ref[0, rows, gs]

        intra, to_state, x_diag = [], [], []
        for hh in range(HEADS_PER_GROUP):
            h = g * HEADS_PER_GROUP + hh
            a_col = jnp.broadcast_to(a_cs[:, h:h + 1], (ch, ch))
            decay_dt = jnp.exp2(jnp.where(causal, a_col - a_src_t[h:h + 1, :], NEG))
            intra.append(cb * decay_dt.astype(BF16))
            to_state.append(b_t * w_bf[h:h + 1, :])
            x_diag.append(x_bf * mask_ref[hh])
        lhs = jnp.concatenate([jnp.concatenate(intra, axis=1), jnp.concatenate(to_state, axis=1)], axis=0)
        r = jnp.dot(lhs, jnp.concatenate(x_diag, axis=0), preferred_element_type=F32)
        y_diag, st_inc = r[:ch], r[ch:]

        a_exp = a_exp_all[:, gs]
        st_old = st_ref[g]
        y_off = jnp.dot(c_bf, st_old.astype(BF16), preferred_element_type=F32) * a_exp
        st_ref[g] = st_old * chunk_decay[:, gs] + st_inc

        y = y_diag + y_off + dskip_ref[:, gs] * x_bf.astype(F32)
        y = y * zs_ref[0, rows, gs].astype(F32)
        y_ref[0, rows, gs] = (y * _rms_scale(y) * nw_ref[:, gs]).astype(BF16)


def _ssd(xc, zs, a_cs, a_src_t, w_t, d_skip_wide, norm_w):
    b, s, _ = xc.shape
    ch = SSM_CHUNK * SSD_SUB
    blk = lambda w: pl.BlockSpec((1, ch, w), lambda bi, c: (bi, c, 0))
    per_chunk = pl.BlockSpec((1, SSD_SUB, SSM_HEADS, SSM_CHUNK), lambda bi, c: (bi, c, 0, 0))
    spread = _head_lane_matrix(SSM_HEADS, LANES)
    masks = _group_head_masks()
    return pl.pallas_call(
        _ssd_kernel,
        grid=(b, s // ch),
        in_specs=[blk(CONV_DIM), blk(SSM_INNER), blk(LANES), per_chunk, per_chunk,
                  _const_spec(spread.shape), _const_spec(masks.shape),
                  _const_spec((1, SSM_INNER)), _const_spec((1, SSM_INNER))],
        out_specs=blk(SSM_INNER),
        out_shape=jax.ShapeDtypeStruct((b, s, SSM_INNER), BF16),
        scratch_shapes=[pltpu.VMEM((SSM_GROUPS, SSM_STATE, GROUP_WIDTH), F32)],
        compiler_params=_params(2),
        name="ssd",
    )(xc, zs, a_cs, a_src_t, w_t, spread, masks, d_skip_wide, norm_w)


_ATT_SLABS = ATT_WIDTH // LANES


def _merge_kernel(x_ref, o1_ref, o4_ref, o16_ref, s1_ref, s4_ref, s16_ref, y_ref, gate_ref,
                  expand_ref, watt_ref, wssm_ref, wout_ref, nw_ref, pre_ref, h_ref, f_ref,
                  onat_ref, snat_ref):
    tm = ROW_TILE
    ssm = jnp.dot(y_ref[...], wssm_ref[...], preferred_element_type=F32)
    for slot, (s_ref, o_ref, d) in enumerate(((s1_ref, o1_ref, 4), (s4_ref, o4_ref, 4),
                                              (s16_ref, o16_ref, 16))):
        for r in range(d):
            snat_ref[slot, pl.ds(r, tm // d, stride=d), :] = s_ref[0, r]
        for slab in range(_ATT_SLABS):
            cols = slice(slab * LANES, (slab + 1) * LANES)
            for r in range(d):
                onat_ref[slot, slab, pl.ds(r, tm // d, stride=d), :] = o_ref[0, r, :, cols].astype(F32)

    stats = [snat_ref[0], snat_ref[1], snat_ref[2]]
    top = jnp.maximum(jnp.maximum(stats[0], stats[1]), stats[2])
    scale = [jnp.exp2(st - top) for st in stats]
    dens = [pltpu.roll(st, LANES - DEN_LANE, axis=1) for st in stats]
    total = scale[0] * dens[0] + scale[1] * dens[1] + scale[2] * dens[2]
    is_head = lax.broadcasted_iota(jnp.int32, (tm, LANES), 1) < N_ATT_HEADS
    inv = 1.0 / jnp.where(is_head, total, 1.0)
    wts = [jnp.where(is_head, sc * inv, 0.0).astype(BF16) for sc in scale]
    slabs = []
    for slab in range(_ATT_SLABS):
        cols = slice(slab * LANES, (slab + 1) * LANES)
        wide = [jnp.dot(w, expand_ref[:, cols], preferred_element_type=F32) for w in wts]
        slabs.append(wide[0] * onat_ref[0, slab] + wide[1] * onat_ref[1, slab]
                     + wide[2] * onat_ref[2, slab])
    att = jnp.concatenate(slabs, axis=1).astype(BF16)
    att = jnp.dot(att, watt_ref[...], preferred_element_type=F32)
    gates = gate_ref[...].astype(F32)
    mixed = gates[:, :D_MODEL] * att + gates[:, D_MODEL:] * ssm
    mixed = jnp.dot(mixed.astype(BF16), wout_ref[...], preferred_element_type=F32)
    h = x_ref[...] + mixed * _rms_scale(mixed) * nw_ref[...]
    h_ref[...] = h
    f_ref[...] = (h * _rms_scale(h) * pre_ref[...]).astype(BF16)


def _merge(x2, outs, stats, y, gates, expand, w_att, w_ssm, w_out, norm_w, ffn_pre_w):
    t = x2.shape[0]
    tm = ROW_TILE
    row = lambda width: pl.BlockSpec((tm, width), lambda i: (i, 0))
    return pl.pallas_call(
        _merge_kernel,
        grid=(t // tm,),
        in_specs=[row(D_MODEL),
                  _residue4_spec(ATT_WIDTH), _residue4_spec(ATT_WIDTH), _residue16_spec(ATT_WIDTH),
                  _residue4_spec(LANES), _residue4_spec(LANES), _residue16_spec(LANES),
                  row(SSM_INNER), row(2 * D_MODEL),
                  _const_spec((LANES, ATT_WIDTH)), _const_spec((ATT_WIDTH, D_MODEL)),
                  _const_spec((SSM_INNER, D_MODEL)), _const_spec((D_MODEL, D_MODEL)),
                  _const_spec((1, D_MODEL)), _const_spec((1, D_MODEL))],
        out_specs=[row(D_MODEL), row(D_MODEL)],
        out_shape=[jax.ShapeDtypeStruct((t, D_MODEL), F32), jax.ShapeDtypeStruct((t, D_MODEL), BF16)],
        scratch_shapes=[pltpu.VMEM((3, _ATT_SLABS, tm, LANES), F32),
                        pltpu.VMEM((3, tm, LANES), F32)],
        compiler_params=_params(1),
        name="merge",
    )(x2, *outs, *stats, y, gates, expand, w_att, w_ssm, w_out, norm_w, ffn_pre_w)


FFN_CHUNK = 1024
FFN_TILE = 1024


def _ffn_kernel(h_ref, f_ref, wup_ref, wdown_ref, post_ref, o_ref):
    acc = jnp.zeros((FFN_TILE, D_MODEL), F32)
    for c in range(0, FFN_HIDDEN, FFN_CHUNK):
        up = jnp.dot(f_ref[...], wup_ref[:, c:c + FFN_CHUNK], preferred_element_type=F32)
        act = jnp.square(jnp.maximum(up, 0.0)).astype(BF16)
        acc = acc + jnp.dot(act, wdown_ref[c:c + FFN_CHUNK, :], preferred_element_type=F32)
    o_ref[...] = h_ref[...] + acc * _rms_scale(acc) * post_ref[...]


def _ffn(h2, f2, w_up, w_down, post_w):
    t = h2.shape[0]
    row = pl.BlockSpec((FFN_TILE, D_MODEL), lambda i: (i, 0))
    return pl.pallas_call(
        _ffn_kernel,
        grid=(t // FFN_TILE,),
        in_specs=[row, row, _const_spec((D_MODEL, FFN_HIDDEN)),
                  _const_spec((FFN_HIDDEN, D_MODEL)), _const_spec((1, D_MODEL))],
        out_specs=row,
        out_shape=jax.ShapeDtypeStruct((t, D_MODEL), F32),
        compiler_params=_params(1),
        name="ffn",
    )(h2, f2, w_up, w_down, post_w)


def _pad_lanes(v):
    return jnp.pad(v.astype(F32), (0, LANES - v.shape[0])).reshape(1, LANES)


PREP_ROWS = 256


def _transpose_cast_kernel(wt_ref, o_ref, *, scale, first, total_rows):
    row = (first + pl.program_id(0)) * PREP_ROWS + lax.broadcasted_iota(jnp.int32, wt_ref.shape, 0)
    w = jnp.where(row < total_rows, wt_ref[...], 0.0)
    o_ref[...] = (w.T * scale).astype(BF16)


def _transpose_cast(wt, row0, n_rows, scale=1.0):
    k = wt.shape[1]
    first = row0 // PREP_ROWS
    assert row0 % PREP_ROWS == 0 and n_rows % PREP_ROWS == 0
    return pl.pallas_call(
        functools.partial(_transpose_cast_kernel, scale=scale, first=first, total_rows=wt.shape[0]),
        grid=(n_rows // PREP_ROWS,),
        in_specs=[pl.BlockSpec((PREP_ROWS, k), lambda j: (first + j, 0))],
        out_specs=pl.BlockSpec((k, PREP_ROWS), lambda j: (0, j)),
        out_shape=jax.ShapeDtypeStruct((k, n_rows), BF16),
        compiler_params=_params(1),
        name="weight_prep",
    )(wt)


def kernel(x, norm_mix_pre_w, w_in, b_gate, conv_w, conv_b, dt_bias, a_log, d_skip, ssm_norm_w,
           w_att_proj, w_ssm_proj, w_out, norm_mix_post_w, norm_ffn_pre_w, w_up, w_down,
           norm_ffn_post_w):
    b, s, d_model = x.shape
    depth = w_in.shape[0]
    assert d_model == D_MODEL and s % SPAN16 == 0
    t = b * s
    expand = _head_lane_matrix(N_ATT_HEADS, LANES)
    h = x.reshape(t, D_MODEL)
    for layer in range(depth):
        w_t_in = jnp.transpose(w_in[layer])
        off_ssm = 3 * ATT_WIDTH
        off_xbc = off_ssm + SSM_INNER
        off_dt = off_xbc + CONV_DIM
        w_qkv = _transpose_cast(w_t_in, 0, off_ssm)
        w_z_half = _transpose_cast(w_t_in, off_ssm, SSM_INNER, 0.5)
        w_xbc = _transpose_cast(w_t_in, off_xbc, CONV_DIM)
        n_tail = 2 * D_MODEL + SSM_HEADS
        w_tail = _transpose_cast(w_t_in, off_dt, -(-n_tail // PREP_ROWS) * PREP_ROWS)
        w_gate_half = 0.5 * w_tail[:, SSM_HEADS:n_tail]
        w_dt_pad = jnp.where(jnp.arange(LANES) < SSM_HEADS, w_tail[:, :LANES], 0).astype(BF16)
        qkv4, qkv16, u = _qkv_proj(h, norm_mix_pre_w[layer].reshape(1, -1), w_qkv)
        zs, xc, gates, a_cs, a_src_t, w_t = _ssm_proj(
            u, w_z_half, w_xbc, w_gate_half, w_dt_pad, 0.5 * b_gate[layer].reshape(1, -1),
            _pad_lanes(dt_bias[layer]),
            0.5 * conv_w[layer], 0.5 * conv_b[layer].reshape(1, -1), _pad_lanes(a_log[layer]), s)
        outs, stats = [], []
        for d in DILATIONS:
            o, st = _attention(qkv16 if d == 16 else qkv4, d, b)
            outs.append(o)
            stats.append(st)
        seq = lambda a: a.reshape(b, s, a.shape[-1])
        per_chunk = lambda a: a.reshape(b, s // SSM_CHUNK, SSM_HEADS, SSM_CHUNK)
        y = _ssd(seq(xc), seq(zs), seq(a_cs), per_chunk(a_src_t), per_chunk(w_t),
                 jnp.repeat(d_skip[layer].astype(F32), HEAD_DIM).reshape(1, -1),
                 ssm_norm_w[layer].reshape(1, -1))
        h, f = _merge(h, outs, stats, y.reshape(t, SSM_INNER), gates, expand,
                      w_att_proj[layer].astype(BF16), w_ssm_proj[layer].astype(BF16),
                      w_out[layer].astype(BF16), norm_mix_post_w[layer].reshape(1, -1),
                      norm_ffn_pre_w[layer].reshape(1, -1))
        h = _ffn(h, f, w_up[layer].astype(BF16), w_down[layer].astype(BF16),
                 norm_ffn_post_w[layer].reshape(1, -1))
    return h.reshape(b, s, D_MODEL)
```

```python
import functools
import math

import numpy as np
import jax
import jax.numpy as jnp
from jax import lax
from jax.experimental import pallas as pl
from jax.experimental.pallas import tpu as pltpu

F32 = jnp.float32
BF16 = jnp.bfloat16

D_MODEL = 1024
HEAD_DIM = 64
N_ATT_HEADS = 12
ATT_WIDTH = N_ATT_HEADS * HEAD_DIM
DILATIONS = (1, 4, 16)
ATT_BLOCK = 128
SSM_INNER = 2048
SSM_HEADS = 32
SSM_GROUPS = 8
SSM_STATE = 128
SSM_CONV = 4
SSM_CHUNK = 128
CONV_DIM = SSM_INNER + 2 * SSM_GROUPS * SSM_STATE
FFN_HIDDEN = 4 * D_MODEL
RMS_EPS = 1e-6
NEG = -1e30

LANES = 128
MXU_N = 256
HEADS_PER_TILE = LANES // HEAD_DIM
N_HEAD_TILES = N_ATT_HEADS // HEADS_PER_TILE
GROUP_WIDTH = SSM_INNER // SSM_GROUPS
HEADS_PER_GROUP = SSM_HEADS // SSM_GROUPS
VMEM_LIMIT_BYTES = 56 * 1024 * 1024

ROW_TILE = 512
SPAN4 = ATT_BLOCK * 4
SPAN16 = ATT_BLOCK * 16
assert ROW_TILE == SPAN4


def _const_spec(shape):
    return pl.BlockSpec(shape, lambda *_: (0,) * len(shape), pipeline_mode=pl.Buffered(1))


def _params(n_axes):
    return pltpu.CompilerParams(dimension_semantics=("arbitrary",) * n_axes,
                                vmem_limit_bytes=VMEM_LIMIT_BYTES)


def _rms_scale(x):
    return lax.rsqrt(jnp.mean(x * x, axis=-1, keepdims=True) + RMS_EPS)


def _residue16_spec(width):
    per_span = SPAN16 // ROW_TILE
    return pl.BlockSpec((1, 16, ROW_TILE // 16, width), lambda i: (i // per_span, 0, i % per_span, 0))


def _residue4_spec(width):
    return pl.BlockSpec((1, 4, ATT_BLOCK, width), lambda i: (i, 0, 0, 0))


_QKV_SLABS = 3 * ATT_WIDTH // LANES
_LOG2E = 1.4426950408889634
Q_SCALE = HEAD_DIM ** -0.5 * _LOG2E


def _qkv_kernel(x_ref, nw_ref, w_ref, *refs):
    d4_ref, d16_ref, u_ref, slab_ref, res4_ref = refs
    tm = ROW_TILE
    x = x_ref[...]
    u_ref[...] = (x * _rms_scale(x) * nw_ref[...]).astype(BF16)
    for a in range(3):
        for c in range(0, ATT_WIDTH, MXU_N):
            acc = jnp.dot(u_ref[...], w_ref[:, a * ATT_WIDTH + c:a * ATT_WIDTH + c + MXU_N],
                          preferred_element_type=F32)
            if a == 0:
                acc = acc * Q_SCALE
            for half in range(MXU_N // LANES):
                slab = (a * ATT_WIDTH + c) // LANES + half
                lane0 = ((a + 2) % 3) * ATT_WIDTH + c + half * LANES
                cols = slice(lane0, lane0 + LANES)
                slab_ref[slab] = acc[:, half * LANES:(half + 1) * LANES]
                for r in range(4):
                    part = slab_ref[slab, pl.ds(r, tm // 4, stride=4), :]
                    d4_ref[0, r, :, cols] = part.astype(BF16)
                    res4_ref[slab, r] = part
                for r in range(16):
                    part = res4_ref[slab, r % 4, pl.ds(r // 4, tm // 16, stride=4), :]
                    d16_ref[0, r, :, cols] = part.astype(BF16)


def _qkv_proj(x2, norm_w, w_qkv):
    t = x2.shape[0]
    tm = ROW_TILE
    shapes = [jax.ShapeDtypeStruct((t // SPAN4, 4, ATT_BLOCK, 3 * ATT_WIDTH), BF16),
              jax.ShapeDtypeStruct((t // SPAN16, 16, ATT_BLOCK, 3 * ATT_WIDTH), BF16)]
    return pl.pallas_call(
        _qkv_kernel,
        grid=(t // tm,),
        in_specs=[pl.BlockSpec((tm, D_MODEL), lambda i: (i, 0)), _const_spec((1, D_MODEL)),
                  _const_spec((D_MODEL, 3 * ATT_WIDTH))],
        out_specs=[_residue4_spec(3 * ATT_WIDTH), _residue16_spec(3 * ATT_WIDTH),
                   pl.BlockSpec((tm, D_MODEL), lambda i: (i, 0))],
        out_shape=shapes + [jax.ShapeDtypeStruct((t, D_MODEL), BF16)],
        scratch_shapes=[pltpu.VMEM((_QKV_SLABS, tm, LANES), F32),
                        pltpu.VMEM((_QKV_SLABS, 4, tm // 4, LANES), F32)],
        compiler_params=_params(1),
        name="qkv_proj",
    )(x2, norm_w, w_qkv)


_HALO = 8
_PAD_BUFFERS = 8
PROJ_N = 1024


def _split3(x):
    hi = x.astype(BF16)
    r = x - hi.astype(F32)
    mid = r.astype(BF16)
    lo = (r - mid.astype(F32)).astype(BF16)
    return hi, mid, lo


def _chunk_tri_matrix():
    i = np.arange(ROW_TILE)
    same_chunk = (i[:, None] // SSM_CHUNK) == (i[None, :] // SSM_CHUNK)
    return jnp.asarray(same_chunk & (i[:, None] >= i[None, :]), dtype=BF16)


def _ssm_proj_kernel(u_ref, wz_ref, wx_ref, wg_ref, wdt_ref, bg_ref, dtb_ref, cw_ref, cb_ref,
                     alog_ref, tri_ref, zs_ref, xc_ref, gate_ref, acs_ref, asrc_ref, wt_ref,
                     halo_ref, pad_ref, *, tiles_per_seq):
    tm = ROW_TILE

    @pl.when(pl.program_id(0) % tiles_per_seq == 0)
    def _():
        halo_ref[...] = jnp.zeros_like(halo_ref)

    def proj(w_ref, off, width):
        return jnp.dot(u_ref[...], w_ref[:, off:off + width], preferred_element_type=F32)

    dt_raw = proj(wdt_ref, 0, LANES) + dtb_ref[...]
    dt = jnp.maximum(dt_raw, 0.0) + jnp.log1p(jnp.exp(-jnp.abs(dt_raw)))
    la = dt * (-_LOG2E * jnp.exp(alog_ref[...]))
    a_cs = sum(jnp.dot(tri_ref[...], part, preferred_element_type=F32) for part in _split3(la))
    acs_ref[...] = a_cs
    ch = SSM_CHUNK
    for cidx in range(tm // ch):
        a_t = a_cs[cidx * ch:(cidx + 1) * ch].T
        dt_t = dt[cidx * ch:(cidx + 1) * ch].T
        asrc_ref[cidx] = (a_t - jnp.log2(dt_t))[:SSM_HEADS]
        wt_ref[cidx] = (jnp.exp2(a_t[:, ch - 1:ch] - a_t) * dt_t)[:SSM_HEADS]

    def z_chunk(c):
        h = proj(wz_ref, c, PROJ_N)
        zs_ref[:, c:c + PROJ_N] = (h + h * jnp.tanh(h)).astype(BF16)

    def gate_chunk(c):
        h = proj(wg_ref, c, PROJ_N) + bg_ref[:, c:c + PROJ_N]
        gate_ref[:, c:c + PROJ_N] = (0.5 + 0.5 * jnp.tanh(h)).astype(BF16)

    def conv_chunk(idx):
        acc2 = proj(wx_ref, idx * PROJ_N, PROJ_N)
        for part in range(PROJ_N // LANES):
            c = idx * PROJ_N + part * LANES
            cs = slice(c, c + LANES)
            buf = (c // LANES) % _PAD_BUFFERS
            acc = acc2[:, part * LANES:(part + 1) * LANES]
            pad_ref[buf, pl.ds(0, _HALO, stride=2), :] = halo_ref[:, cs]
            pad_ref[buf, pl.ds(2 * _HALO, tm, stride=2), :] = acc
            halo_ref[:, cs] = acc[tm - _HALO:, :]
            h = cb_ref[:, cs] + cw_ref[SSM_CONV - 1:SSM_CONV, cs] * acc
            for tap in range(SSM_CONV - 1):
                start = 2 * (_HALO - (SSM_CONV - 1) + tap)
                h = h + cw_ref[tap:tap + 1, cs] * pad_ref[buf, pl.ds(start, tm, stride=2), :]
            xc_ref[:, cs] = (h + h * jnp.tanh(h)).astype(BF16)

    light = ([functools.partial(z_chunk, c) for c in range(0, SSM_INNER, PROJ_N)]
             + [functools.partial(gate_chunk, c) for c in range(0, 2 * D_MODEL, PROJ_N)])
    for idx in range(CONV_DIM // PROJ_N):
        conv_chunk(idx)
        light[idx]()


def _ssm_proj(u, w_z_half, w_xbc, w_gate, w_dt_pad, b_gate, dt_bias_pad, conv_w_half, conv_b_half,
              a_log_pad, seq_len):
    t = u.shape[0]
    tm = ROW_TILE
    row = lambda width: pl.BlockSpec((tm, width), lambda i: (i, 0))
    per_chunk = pl.BlockSpec((tm // SSM_CHUNK, SSM_HEADS, SSM_CHUNK), lambda i: (i, 0, 0))
    widths = (SSM_INNER, CONV_DIM, 2 * D_MODEL)
    out_shape = [jax.ShapeDtypeStruct((t, w), BF16) for w in widths]
    out_shape.append(jax.ShapeDtypeStruct((t, LANES), F32))
    out_shape += [jax.ShapeDtypeStruct((t // SSM_CHUNK, SSM_HEADS, SSM_CHUNK), F32)] * 2
    tri = _chunk_tri_matrix()
    return pl.pallas_call(
        functools.partial(_ssm_proj_kernel, tiles_per_seq=seq_len // tm),
        grid=(t // tm,),
        in_specs=[row(D_MODEL), _const_spec((D_MODEL, SSM_INNER)), _const_spec((D_MODEL, CONV_DIM)),
                  _const_spec((D_MODEL, 2 * D_MODEL)), _const_spec((D_MODEL, LANES)),
                  _const_spec((1, 2 * D_MODEL)), _const_spec((1, LANES)),
                  _const_spec((SSM_CONV, CONV_DIM)), _const_spec((1, CONV_DIM)),
                  _const_spec((1, LANES)), _const_spec(tri.shape)],
        out_specs=[row(w) for w in widths] + [row(LANES), per_chunk, per_chunk],
        out_shape=out_shape,
        scratch_shapes=[pltpu.VMEM((_HALO, CONV_DIM), F32),
                        pltpu.VMEM((_PAD_BUFFERS, 2 * (_HALO + tm), LANES), F32)],
        compiler_params=_params(1),
        name="ssm_proj",
    )(u, w_z_half, w_xbc, w_gate, w_dt_pad, b_gate, dt_bias_pad, conv_w_half, conv_b_half,
      a_log_pad, tri)


ATT_SUB = 16
ATT_SUB16 = 16


def _alibi_slopes(n):
    def pow2(m):
        start = 2.0 ** (-8.0 / m)
        return [start ** (i + 1) for i in range(m)]
    if (n & (n - 1)) == 0:
        s = pow2(n)
    else:
        c = 2 ** int(math.floor(math.log2(n)))
        s = pow2(c) + pow2(2 * c)[0::2][: n - c]
    return np.array(s, dtype=np.float32)


def _attention_bias(dilation, interleave):
    blk = ATT_BLOCK
    seg = blk // interleave
    pos = interleave * (np.arange(blk) % seg) + np.arange(blk) // seg
    i = pos[:, None]
    j = pos[None, :]
    dist_prev = blk + i - j
    dist_cur = i - j
    slopes = (_alibi_slopes(N_ATT_HEADS) * np.float32(_LOG2E)).reshape(N_HEAD_TILES, HEADS_PER_TILE)
    out = np.empty((2, N_HEAD_TILES, HEADS_PER_TILE * blk, 2 * blk), np.float32)
    for tile in range(N_HEAD_TILES):
        for hh in range(HEADS_PER_TILE):
            s = slopes[tile, hh]
            prev = np.where(dist_prev <= blk, -s * (dist_prev * dilation).astype(np.float32), NEG)
            cur = np.where(dist_cur >= 0, -s * (dist_cur * dilation).astype(np.float32), NEG)
            rows = slice(hh * blk, (hh + 1) * blk)
            out[1, tile, rows, :blk] = prev
            out[0, tile, rows, :blk] = NEG
            out[:, tile, rows, blk:] = cur
    return jnp.asarray(out)


DEN_LANE = 16
assert N_ATT_HEADS <= DEN_LANE


def _attention_kernel(cur_ref, prev_ref, bias_ref, o_ref, st_ref, *, dilation):
    blk = ATT_BLOCK
    seg = blk // 4
    n_spans, n_res = cur_ref.shape[0], cur_ref.shape[1]
    n = pl.program_id(1)
    lane = lax.broadcasted_iota(jnp.int32, (blk, LANES), 1)
    in_head_a = lane < HEAD_DIM
    keep_a = jnp.where(in_head_a, 1.0, 0.0).astype(BF16)
    keep_b = jnp.where(in_head_a, 0.0, 1.0).astype(BF16)
    ones = jnp.ones((2 * blk, LANES), BF16)

    def block(ref, sp, g, cols):
        if dilation == 1:
            return jnp.concatenate([ref[sp, r, g * seg:(g + 1) * seg, cols] for r in range(4)], axis=0)
        return ref[sp, g, :, cols]

    def store(ref, sp, g, cols, value):
        if dilation == 1:
            for r in range(4):
                ref[sp, r, g * seg:(g + 1) * seg, cols] = value[r * seg:(r + 1) * seg]
        else:
            ref[sp, g, :, cols] = value

    def previous(sp, g, cols):
        if dilation == 1:
            if g > 0:
                return block(cur_ref, sp, g - 1, cols), None
            if sp > 0:
                return block(cur_ref, sp - 1, n_res - 1, cols), None
            return jnp.concatenate([prev_ref[0, r, :, cols] for r in range(4)], axis=0), n > 0
        if sp > 0:
            return cur_ref[sp - 1, g, :, cols], None
        return prev_ref[0, g, :, cols], n > 0

    for sp, g in [(sp, g) for sp in range(n_spans) for g in range(n_res)]:
        stats = jnp.zeros((blk, LANES), F32)
        for tile in range(N_HEAD_TILES):
            cols = slice(tile * LANES, (tile + 1) * LANES)
            k_cols = cols
            v_cols = slice(ATT_WIDTH + tile * LANES, ATT_WIDTH + (tile + 1) * LANES)
            q_cols = slice(2 * ATT_WIDTH + tile * LANES, 2 * ATT_WIDTH + (tile + 1) * LANES)
            k_prev, has_prev = previous(sp, g, k_cols)
            v_prev, _ = previous(sp, g, v_cols)
            q = block(cur_ref, sp, g, q_cols)
            q2 = jnp.concatenate([q * keep_a, q * keep_b], axis=0)
            k2 = jnp.concatenate([k_prev, block(cur_ref, sp, g, k_cols)], axis=0)
            s = lax.dot_general(q2, k2, (((1,), (1,)), ((), ())), preferred_element_type=F32)
            if has_prev is None:
                s = s + bias_ref[1, tile]
            else:
                s = s + bias_ref[has_prev.astype(jnp.int32), tile]
            m = jnp.max(s, axis=-1, keepdims=True)
            p = jnp.exp2(s - m).astype(BF16)
            v2 = jnp.concatenate([v_prev, block(cur_ref, sp, g, v_cols)], axis=0)
            r = jnp.dot(p, jnp.concatenate([v2, ones], axis=1), preferred_element_type=F32)
            acc, den = r[:, :LANES], r[:, LANES:]
            store(o_ref, sp, g, cols, jnp.where(in_head_a, acc[:blk], acc[blk:]).astype(BF16))
            for hh in range(HEADS_PER_TILE):
                head = tile * HEADS_PER_TILE + hh
                stats = jnp.where(lane == head, m[hh * blk:(hh + 1) * blk], stats)
                stats = jnp.where(lane == DEN_LANE + head, den[hh * blk:(hh + 1) * blk], stats)
        store(st_ref, sp, g, slice(None), stats)


def _attention(qkv, dilation, batch):
    spans, residues, blk, _ = qkv.shape
    per_batch = spans // batch
    d = dilation
    prev_rows = blk
    if d == 16:
        step_spans, step_res = 1, ATT_SUB16
        grid = (batch, per_batch, residues // step_res)
        cur_map = lambda bi, n, rg: (bi * per_batch + n, rg, 0, 0)
        prev_map = lambda bi, n, rg: (bi * per_batch + jnp.maximum(n - 1, 0), rg, 0, 0)
    else:
        step_spans, step_res = ATT_SUB // residues, residues
        steps = per_batch // step_spans
        grid = (batch, steps)
        cur_map = lambda bi, n: (bi * steps + n, 0, 0, 0)
        last_span = lambda bi, n: bi * per_batch + jnp.maximum(step_spans * n - 1, 0)
        if d == 1:
            prev_rows = blk // 4
            prev_map = lambda bi, n: (last_span(bi, n), 0, 3, 0)
        else:
            prev_map = lambda bi, n: (last_span(bi, n), 0, 0, 0)
    cur = lambda w: pl.BlockSpec((step_spans, step_res, blk, w), cur_map)
    prev = lambda w: pl.BlockSpec((1, step_res, prev_rows, w), prev_map)
    bias = _attention_bias(d, 4 if d == 1 else 1)
    return pl.pallas_call(
        functools.partial(_attention_kernel, dilation=d),
        grid=grid,
        in_specs=[cur(3 * ATT_WIDTH), prev(2 * ATT_WIDTH), _const_spec(bias.shape)],
        out_specs=[cur(ATT_WIDTH), cur(LANES)],
        out_shape=[jax.ShapeDtypeStruct((spans, residues, blk, ATT_WIDTH), BF16),
                   jax.ShapeDtypeStruct((spans, residues, blk, LANES), F32)],
        compiler_params=_params(len(grid)),
        name=f"attention_d{d}",
    )(qkv, qkv, bias)


SSD_SUB = 8


def _head_lane_matrix(n_heads, rows):
    e = np.zeros((rows, n_heads * HEAD_DIM), np.float32)
    for h in range(n_heads):
        e[h, h * HEAD_DIM:(h + 1) * HEAD_DIM] = 1.0
    return jnp.asarray(e, dtype=BF16)


def _group_head_masks():
    m = np.zeros((HEADS_PER_GROUP, SSM_CHUNK, GROUP_WIDTH), np.float32)
    for hh in range(HEADS_PER_GROUP):
        m[hh, :, hh * HEAD_DIM:(hh + 1) * HEAD_DIM] = 1.0
    return jnp.asarray(m, dtype=BF16)


def _ssd_kernel(xc_ref, zs_ref, acs_ref, asrc_ref, wt_ref, spread_ref, mask_ref, dskip_ref, nw_ref,
                y_ref, st_ref):
    @pl.when(pl.program_id(1) == 0)
    def _():
        st_ref[...] = jnp.zeros_like(st_ref)

    for sub in range(SSD_SUB):
        _ssd_chunk(sub, xc_ref, zs_ref, acs_ref, asrc_ref, wt_ref, spread_ref, mask_ref, dskip_ref,
                   nw_ref, y_ref, st_ref)


def _ssd_chunk(sub, xc_ref, zs_ref, acs_ref, asrc_ref, wt_ref, spread_ref, mask_ref, dskip_ref,
               nw_ref, y_ref, st_ref):
    ch = SSM_CHUNK
    rows = slice(sub * ch, (sub + 1) * ch)
    a_cs = acs_ref[0, rows, :]
    a_src_t = asrc_ref[0, sub]
    w_bf = wt_ref[0, sub].astype(BF16)
    row_i = lax.broadcasted_iota(jnp.int32, (ch, ch), 0)
    col_i = lax.broadcasted_iota(jnp.int32, (ch, ch), 1)
    causal = row_i >= col_i
    e = jnp.exp2(a_cs)
    a_exp_all = jnp.dot(e.astype(BF16), spread_ref[...], preferred_element_type=F32)
    e_end = jnp.broadcast_to(e[ch - 1:ch, :], (8, LANES))
    e_end_hi = e_end.astype(BF16)
    e_end_lo = (e_end - e_end_hi.astype(F32)).astype(BF16)
    chunk_decay = (jnp.dot(e_end_hi, spread_ref[...], preferred_element_type=F32)
                   + jnp.dot(e_end_lo, spread_ref[...], preferred_element_type=F32))[0:1]

    for g in range(SSM_GROUPS):
        gs = slice(g * GROUP_WIDTH, (g + 1) * GROUP_WIDTH)
        b_bf = xc_ref[0, rows, SSM_INNER + g * SSM_STATE:SSM_INNER + (g + 1) * SSM_STATE]
        c_bf = xc_ref[0, rows, SSM_INNER + (SSM_GROUPS + g) * SSM_STATE:
                      SSM_INNER + (SSM_GROUPS + g + 1) * SSM_STATE]
        cb = lax.dot_general(c_bf, b_bf, (((1,), (1,)), ((), ())),
                             preferred_element_type=F32).astype(BF16)
        b_t = b_bf.astype(F32).T.astype(BF16)
        x_bf = xc_ref[0, rows, gs]

        intra, to_state, x_diag = [], [], []
        for hh in range(HEADS_PER_GROUP):
            h = g * HEADS_PER_GROUP + hh
            a_col = jnp.broadcast_to(a_cs[:, h:h + 1], (ch, ch))
            decay_dt = jnp.exp2(jnp.where(causal, a_col - a_src_t[h:h + 1, :], NEG))
            intra.append(cb * decay_dt.astype(BF16))
            to_state.append(b_t * w_bf[h:h + 1, :])
            x_diag.append(x_bf * mask_ref[hh])
        lhs = jnp.concatenate([jnp.concatenate(intra, axis=1), jnp.concatenate(to_state, axis=1)], axis=0)
        r = jnp.dot(lhs, jnp.concatenate(x_diag, axis=0), preferred_element_type=F32)
        y_diag, st_inc = r[:ch], r[ch:]

        a_exp = a_exp_all[:, gs]
        st_old = st_ref[g]
        y_off = jnp.dot(c_bf, st_old.astype(BF16), preferred_element_type=F32) * a_exp
        st_ref[g] = st_old * chunk_decay[:, gs] + st_inc

        y = y_diag + y_off + dskip_ref[:, gs] * x_bf.astype(F32)
        y = y * zs_ref[0, rows, gs].astype(F32)
        y_ref[0, rows, gs] = (y * _rms_scale(y) * nw_ref[:, gs]).astype(BF16)


def _ssd(xc, zs, a_cs, a_src_t, w_t, d_skip_wide, norm_w):
    b, s, _ = xc.shape
    ch = SSM_CHUNK * SSD_SUB
    blk = lambda w: pl.BlockSpec((1, ch, w), lambda bi, c: (bi, c, 0))
    per_chunk = pl.BlockSpec((1, SSD_SUB, SSM_HEADS, SSM_CHUNK), lambda bi, c: (bi, c, 0, 0))
    spread = _head_lane_matrix(SSM_HEADS, LANES)
    masks = _group_head_masks()
    return pl.pallas_call(
        _ssd_kernel,
        grid=(b, s // ch),
        in_specs=[blk(CONV_DIM), blk(SSM_INNER), blk(LANES), per_chunk, per_chunk,
                  _const_spec(spread.shape), _const_spec(masks.shape),
                  _const_spec((1, SSM_INNER)), _const_spec((1, SSM_INNER))],
        out_specs=blk(SSM_INNER),
        out_shape=jax.ShapeDtypeStruct((b, s, SSM_INNER), BF16),
        scratch_shapes=[pltpu.VMEM((SSM_GROUPS, SSM_STATE, GROUP_WIDTH), F32)],
        compiler_params=_params(2),
        name="ssd",
    )(xc, zs, a_cs, a_src_t, w_t, spread, masks, d_skip_wide, norm_w)


_ATT_SLABS = ATT_WIDTH // LANES


def _merge_kernel(x_ref, o1_ref, o4_ref, o16_ref, s1_ref, s4_ref, s16_ref, y_ref, gate_ref,
                  expand_ref, watt_ref, wssm_ref, wout_ref, nw_ref, pre_ref, h_ref, f_ref,
                  onat_ref, snat_ref, tmp_ref):
    tm = ROW_TILE
    ssm = jnp.dot(y_ref[...], wssm_ref[...], preferred_element_type=F32)
    for slot, (s_ref, o_ref, d) in enumerate(((s1_ref, o1_ref, 4), (s4_ref, o4_ref, 4),
                                              (s16_ref, o16_ref, 16))):
        for r in range(d):
            snat_ref[slot, pl.ds(r, tm // d, stride=d), :] = s_ref[0, r]
        for slab in range(_ATT_SLABS):
            cols = slice(slab * LANES, (slab + 1) * LANES)
            if d == 4:
                for r in range(d):
                    onat_ref[slot, slab, pl.ds(r, tm // d, stride=d), :] = o_ref[0, r, :, cols].astype(F32)
            else:
                for r in range(d):
                    tmp_ref[slab, r % 4, pl.ds(r // 4, tm // d, stride=4), :] = (
                        o_ref[0, r, :, cols].astype(F32))
                for r4 in range(4):
                    onat_ref[slot, slab, pl.ds(r4, tm // 4, stride=4), :] = tmp_ref[slab, r4]

    stats = [snat_ref[0], snat_ref[1], snat_ref[2]]
    top = jnp.maximum(jnp.maximum(stats[0], stats[1]), stats[2])
    scale = [jnp.exp2(st - top) for st in stats]
    dens = [pltpu.roll(st, LANES - DEN_LANE, axis=1) for st in stats]
    total = scale[0] * dens[0] + scale[1] * dens[1] + scale[2] * dens[2]
    is_head = lax.broadcasted_iota(jnp.int32, (tm, LANES), 1) < N_ATT_HEADS
    inv = 1.0 / jnp.where(is_head, total, 1.0)
    wts = [jnp.where(is_head, sc * inv, 0.0).astype(BF16) for sc in scale]
    slabs = []
    for slab in range(_ATT_SLABS):
        cols = slice(slab * LANES, (slab + 1) * LANES)
        wide = [jnp.dot(w, expand_ref[:, cols], preferred_element_type=F32) for w in wts]
        slabs.append(wide[0] * onat_ref[0, slab] + wide[1] * onat_ref[1, slab]
                     + wide[2] * onat_ref[2, slab])
    att = jnp.concatenate(slabs, axis=1).astype(BF16)
    att = jnp.dot(att, watt_ref[...], preferred_element_type=F32)
    gates = gate_ref[...].astype(F32)
    mixed = gates[:, :D_MODEL] * att + gates[:, D_MODEL:] * ssm
    mixed = jnp.dot(mixed.astype(BF16), wout_ref[...], preferred_element_type=F32)
    h = x_ref[...] + mixed * _rms_scale(mixed) * nw_ref[...]
    h_ref[...] = h
    f_ref[...] = (h * _rms_scale(h) * pre_ref[...]).astype(BF16)


def _merge(x2, outs, stats, y, gates, expand, w_att, w_ssm, w_out, norm_w, ffn_pre_w):
    t = x2.shape[0]
    tm = ROW_TILE
    row = lambda width: pl.BlockSpec((tm, width), lambda i: (i, 0))
    return pl.pallas_call(
        _merge_kernel,
        grid=(t // tm,),
        in_specs=[row(D_MODEL),
                  _residue4_spec(ATT_WIDTH), _residue4_spec(ATT_WIDTH), _residue16_spec(ATT_WIDTH),
                  _residue4_spec(LANES), _residue4_spec(LANES), _residue16_spec(LANES),
                  row(SSM_INNER), row(2 * D_MODEL),
                  _const_spec((LANES, ATT_WIDTH)), _const_spec((ATT_WIDTH, D_MODEL)),
                  _const_spec((SSM_INNER, D_MODEL)), _const_spec((D_MODEL, D_MODEL)),
                  _const_spec((1, D_MODEL)), _const_spec((1, D_MODEL))],
        out_specs=[row(D_MODEL), row(D_MODEL)],
        out_shape=[jax.ShapeDtypeStruct((t, D_MODEL), F32), jax.ShapeDtypeStruct((t, D_MODEL), BF16)],
        scratch_shapes=[pltpu.VMEM((3, _ATT_SLABS, tm, LANES), F32),
                        pltpu.VMEM((3, tm, LANES), F32),
                        pltpu.VMEM((_ATT_SLABS, 4, tm // 4, LANES), F32)],
        compiler_params=_params(1),
        name="merge",
    )(x2, *outs, *stats, y, gates, expand, w_att, w_ssm, w_out, norm_w, ffn_pre_w)


FFN_CHUNK = 1024
FFN_TILE = 1024


def _ffn_kernel(h_ref, f_ref, wup_ref, wdown_ref, post_ref, o_ref):
    acc = jnp.zeros((FFN_TILE, D_MODEL), F32)
    for c in range(0, FFN_HIDDEN, FFN_CHUNK):
        up = jnp.dot(f_ref[...], wup_ref[:, c:c + FFN_CHUNK], preferred_element_type=F32)
        act = jnp.square(jnp.maximum(up, 0.0)).astype(BF16)
        acc = acc + jnp.dot(act, wdown_ref[c:c + FFN_CHUNK, :], preferred_element_type=F32)
    o_ref[...] = h_ref[...] + acc * _rms_scale(acc) * post_ref[...]


def _ffn(h2, f2, w_up, w_down, post_w):
    t = h2.shape[0]
    row = pl.BlockSpec((FFN_TILE, D_MODEL), lambda i: (i, 0))
    return pl.pallas_call(
        _ffn_kernel,
        grid=(t // FFN_TILE,),
        in_specs=[row, row, _const_spec((D_MODEL, FFN_HIDDEN)),
                  _const_spec((FFN_HIDDEN, D_MODEL)), _const_spec((1, D_MODEL))],
        out_specs=row,
        out_shape=jax.ShapeDtypeStruct((t, D_MODEL), F32),
        compiler_params=_params(1),
        name="ffn",
    )(h2, f2, w_up, w_down, post_w)


def _pad_lanes(v):
    return jnp.pad(v.astype(F32), (0, LANES - v.shape[0])).reshape(1, LANES)


PREP_ROWS = 256


def _transpose_cast_kernel(wt_ref, o_ref, *, scale, first, total_rows):
    row = (first + pl.program_id(0)) * PREP_ROWS + lax.broadcasted_iota(jnp.int32, wt_ref.shape, 0)
    w = jnp.where(row < total_rows, wt_ref[...], 0.0)
    o_ref[...] = (w.T * scale).astype(BF16)


def _transpose_cast(wt, row0, n_rows, scale=1.0):
    k = wt.shape[1]
    first = row0 // PREP_ROWS
    assert row0 % PREP_ROWS == 0 and n_rows % PREP_ROWS == 0
    return pl.pallas_call(
        functools.partial(_transpose_cast_kernel, scale=scale, first=first, total_rows=wt.shape[0]),
        grid=(n_rows // PREP_ROWS,),
        in_specs=[pl.BlockSpec((PREP_ROWS, k), lambda j: (first + j, 0))],
        out_specs=pl.BlockSpec((k, PREP_ROWS), lambda j: (0, j)),
        out_shape=jax.ShapeDtypeStruct((k, n_rows), BF16),
        compiler_params=_params(1),
        name="weight_prep",
    )(wt)


def kernel(x, norm_mix_pre_w, w_in, b_gate, conv_w, conv_b, dt_bias, a_log, d_skip, ssm_norm_w,
           w_att_proj, w_ssm_proj, w_out, norm_mix_post_w, norm_ffn_pre_w, w_up, w_down,
           norm_ffn_post_w):
    b, s, d_model = x.shape
    depth = w_in.shape[0]
    assert d_model == D_MODEL and s % SPAN16 == 0
    t = b * s
    expand = _head_lane_matrix(N_ATT_HEADS, LANES)
    h = x.reshape(t, D_MODEL)
    for layer in range(depth):
        w_t_in = jnp.transpose(w_in[layer])
        off_ssm = 3 * ATT_WIDTH
        off_xbc = off_ssm + SSM_INNER
        off_dt = off_xbc + CONV_DIM
        w_qkv = _transpose_cast(w_t_in, 0, off_ssm)
        w_z_half = _transpose_cast(w_t_in, off_ssm, SSM_INNER, 0.5)
        w_xbc = _transpose_cast(w_t_in, off_xbc, CONV_DIM)
        n_tail = 2 * D_MODEL + SSM_HEADS
        w_tail = _transpose_cast(w_t_in, off_dt, -(-n_tail // PREP_ROWS) * PREP_ROWS)
        w_gate_half = 0.5 * w_tail[:, SSM_HEADS:n_tail]
        w_dt_pad = jnp.where(jnp.arange(LANES) < SSM_HEADS, w_tail[:, :LANES], 0).astype(BF16)
        qkv4, qkv16, u = _qkv_proj(h, norm_mix_pre_w[layer].reshape(1, -1), w_qkv)
        zs, xc, gates, a_cs, a_src_t, w_t = _ssm_proj(
            u, w_z_half, w_xbc, w_gate_half, w_dt_pad, 0.5 * b_gate[layer].reshape(1, -1),
            _pad_lanes(dt_bias[layer]),
            0.5 * conv_w[layer], 0.5 * conv_b[layer].reshape(1, -1), _pad_lanes(a_log[layer]), s)
        outs, stats = [], []
        for d in DILATIONS:
            o, st = _attention(qkv16 if d == 16 else qkv4, d, b)
            outs.append(o)
            stats.append(st)
        seq = lambda a: a.reshape(b, s, a.shape[-1])
        per_chunk = lambda a: a.reshape(b, s // SSM_CHUNK, SSM_HEADS, SSM_CHUNK)
        y = _ssd(seq(xc), seq(zs), seq(a_cs), per_chunk(a_src_t), per_chunk(w_t),
                 jnp.repeat(d_skip[layer].astype(F32), HEAD_DIM).reshape(1, -1),
                 ssm_norm_w[layer].reshape(1, -1))
        h, f = _merge(h, outs, stats, y.reshape(t, SSM_INNER), gates, expand,
                      w_att_proj[layer].astype(BF16), w_ssm_proj[layer].astype(BF16),
                      w_out[layer].astype(BF16), norm_mix_post_w[layer].reshape(1, -1),
                      norm_ffn_pre_w[layer].reshape(1, -1))
        h = _ffn(h, f, w_up[layer].astype(BF16), w_down[layer].astype(BF16),
                 norm_ffn_post_w[layer].reshape(1, -1))
    return h.reshape(b, s, D_MODEL)
```

```python
import functools
import math

import numpy as np
import jax
import jax.numpy as jnp
from jax import lax
from jax.experimental import pallas as pl
from jax.experimental.pallas import tpu as pltpu

F32 = jnp.float32
BF16 = jnp.bfloat16

D_MODEL = 1024
HEAD_DIM = 64
N_ATT_HEADS = 12
ATT_WIDTH = N_ATT_HEADS * HEAD_DIM
DILATIONS = (1, 4, 16)
ATT_BLOCK = 128
SSM_INNER = 2048
SSM_HEADS = 32
SSM_GROUPS = 8
SSM_STATE = 128
SSM_CONV = 4
SSM_CHUNK = 128
CONV_DIM = SSM_INNER + 2 * SSM_GROUPS * SSM_STATE
FFN_HIDDEN = 4 * D_MODEL
RMS_EPS = 1e-6
NEG = -1e30

LANES = 128
MXU_N = 256
HEADS_PER_TILE = LANES // HEAD_DIM
N_HEAD_TILES = N_ATT_HEADS // HEADS_PER_TILE
GROUP_WIDTH = SSM_INNER // SSM_GROUPS
HEADS_PER_GROUP = SSM_HEADS // SSM_GROUPS
VMEM_LIMIT_BYTES = 56 * 1024 * 1024

ROW_TILE = 512
SPAN4 = ATT_BLOCK * 4
SPAN16 = ATT_BLOCK * 16
assert ROW_TILE == SPAN4


def _const_spec(shape):
    return pl.BlockSpec(shape, lambda *_: (0,) * len(shape), pipeline_mode=pl.Buffered(1))


def _params(n_axes):
    return pltpu.CompilerParams(dimension_semantics=("arbitrary",) * n_axes,
                                vmem_limit_bytes=VMEM_LIMIT_BYTES)


def _rms_scale(x):
    return lax.rsqrt(jnp.mean(x * x, axis=-1, keepdims=True) + RMS_EPS)


def _residue16_spec(width):
    per_span = SPAN16 // ROW_TILE
    return pl.BlockSpec((1, 16, ROW_TILE // 16, width), lambda i: (i // per_span, 0, i % per_span, 0))


def _residue4_spec(width):
    return pl.BlockSpec((1, 4, ATT_BLOCK, width), lambda i: (i, 0, 0, 0))


_QKV_SLABS = 3 * ATT_WIDTH // LANES
_LOG2E = 1.4426950408889634
Q_SCALE = HEAD_DIM ** -0.5 * _LOG2E


def _qkv_kernel(x_ref, nw_ref, w_ref, *refs):
    d4_ref, d16_ref, u_ref, slab_ref, res4_ref = refs
    tm = ROW_TILE
    x = x_ref[...]
    u_ref[...] = (x * _rms_scale(x) * nw_ref[...]).astype(BF16)
    for a in range(3):
        for c in range(0, ATT_WIDTH, MXU_N):
            acc = jnp.dot(u_ref[...], w_ref[:, a * ATT_WIDTH + c:a * ATT_WIDTH + c + MXU_N],
                          preferred_element_type=F32)
            if a == 0:
                acc = acc * Q_SCALE
            for half in range(MXU_N // LANES):
                slab = (a * ATT_WIDTH + c) // LANES + half
                lane0 = ((a + 2) % 3) * ATT_WIDTH + c + half * LANES
                cols = slice(lane0, lane0 + LANES)
                slab_ref[slab] = acc[:, half * LANES:(half + 1) * LANES]
                for r in range(4):
                    part = slab_ref[slab, pl.ds(r, tm // 4, stride=4), :]
                    d4_ref[0, r, :, cols] = part.astype(BF16)
                    res4_ref[slab, r] = part
                for r in range(16):
                    part = res4_ref[slab, r % 4, pl.ds(r // 4, tm // 16, stride=4), :]
                    d16_ref[0, r, :, cols] = part.astype(BF16)


def _qkv_proj(x2, norm_w, w_qkv):
    t = x2.shape[0]
    tm = ROW_TILE
    shapes = [jax.ShapeDtypeStruct((t // SPAN4, 4, ATT_BLOCK, 3 * ATT_WIDTH), BF16),
              jax.ShapeDtypeStruct((t // SPAN16, 16, ATT_BLOCK, 3 * ATT_WIDTH), BF16)]
    return pl.pallas_call(
        _qkv_kernel,
        grid=(t // tm,),
        in_specs=[pl.BlockSpec((tm, D_MODEL), lambda i: (i, 0)), _const_spec((1, D_MODEL)),
                  _const_spec((D_MODEL, 3 * ATT_WIDTH))],
        out_specs=[_residue4_spec(3 * ATT_WIDTH), _residue16_spec(3 * ATT_WIDTH),
                   pl.BlockSpec((tm, D_MODEL), lambda i: (i, 0))],
        out_shape=shapes + [jax.ShapeDtypeStruct((t, D_MODEL), BF16)],
        scratch_shapes=[pltpu.VMEM((_QKV_SLABS, tm, LANES), F32),
                        pltpu.VMEM((_QKV_SLABS, 4, tm // 4, LANES), F32)],
        compiler_params=_params(1),
        name="qkv_proj",
    )(x2, norm_w, w_qkv)


_HALO = 8
_PAD_BUFFERS = 8
PROJ_N = 1024


def _split3(x):
    hi = x.astype(BF16)
    r = x - hi.astype(F32)
    mid = r.astype(BF16)
    lo = (r - mid.astype(F32)).astype(BF16)
    return hi, mid, lo


def _chunk_tri_matrix():
    i = np.arange(ROW_TILE)
    same_chunk = (i[:, None] // SSM_CHUNK) == (i[None, :] // SSM_CHUNK)
    return jnp.asarray(same_chunk & (i[:, None] >= i[None, :]), dtype=BF16)


def _ssm_proj_kernel(u_ref, wz_ref, wx_ref, wg_ref, wdt_ref, bg_ref, dtb_ref, cw_ref, cb_ref,
                     alog_ref, tri_ref, zs_ref, xc_ref, gate_ref, acs_ref, asrc_ref, wt_ref,
                     halo_ref, pad_ref, *, tiles_per_seq):
    tm = ROW_TILE

    @pl.when(pl.program_id(0) % tiles_per_seq == 0)
    def _():
        halo_ref[...] = jnp.zeros_like(halo_ref)

    def proj(w_ref, off, width):
        return jnp.dot(u_ref[...], w_ref[:, off:off + width], preferred_element_type=F32)

    dt_raw = proj(wdt_ref, 0, LANES) + dtb_ref[...]
    dt = jnp.maximum(dt_raw, 0.0) + jnp.log1p(jnp.exp(-jnp.abs(dt_raw)))
    la = dt * (-_LOG2E * jnp.exp(alog_ref[...]))
    a_cs = sum(jnp.dot(tri_ref[...], part, preferred_element_type=F32) for part in _split3(la))
    acs_ref[...] = a_cs
    ch = SSM_CHUNK
    for cidx in range(tm // ch):
        a_t = a_cs[cidx * ch:(cidx + 1) * ch].T
        dt_t = dt[cidx * ch:(cidx + 1) * ch].T
        asrc_ref[cidx] = (a_t - jnp.log2(dt_t))[:SSM_HEADS]
        wt_ref[cidx] = (jnp.exp2(a_t[:, ch - 1:ch] - a_t) * dt_t)[:SSM_HEADS]

    def z_chunk(c):
        h = proj(wz_ref, c, PROJ_N)
        zs_ref[:, c:c + PROJ_N] = (h + h * jnp.tanh(h)).astype(BF16)

    def gate_chunk(c):
        h = proj(wg_ref, c, PROJ_N) + bg_ref[:, c:c + PROJ_N]
        gate_ref[:, c:c + PROJ_N] = (0.5 + 0.5 * jnp.tanh(h)).astype(BF16)

    def conv_chunk(idx):
        acc2 = proj(wx_ref, idx * PROJ_N, PROJ_N)
        for part in range(PROJ_N // LANES):
            c = idx * PROJ_N + part * LANES
            cs = slice(c, c + LANES)
            buf = (c // LANES) % _PAD_BUFFERS
            acc = acc2[:, part * LANES:(part + 1) * LANES]
            pad_ref[buf, pl.ds(0, _HALO, stride=2), :] = halo_ref[:, cs]
            pad_ref[buf, pl.ds(2 * _HALO, tm, stride=2), :] = acc
            halo_ref[:, cs] = acc[tm - _HALO:, :]
            cw = cw_ref[:, cs].astype(BF16)
            h = cb_ref[:, cs].astype(BF16) + cw[SSM_CONV - 1:SSM_CONV] * acc.astype(BF16)
            for tap in range(SSM_CONV - 1):
                start = 2 * (_HALO - (SSM_CONV - 1) + tap)
                h = h + cw[tap:tap + 1] * pad_ref[buf, pl.ds(start, tm, stride=2), :].astype(BF16)
            xc_ref[:, cs] = h + h * jnp.tanh(h)

    light = ([functools.partial(z_chunk, c) for c in range(0, SSM_INNER, PROJ_N)]
             + [functools.partial(gate_chunk, c) for c in range(0, 2 * D_MODEL, PROJ_N)])
    for idx in range(CONV_DIM // PROJ_N):
        conv_chunk(idx)
        light[idx]()


def _ssm_proj(u, w_z_half, w_xbc, w_gate, w_dt_pad, b_gate, dt_bias_pad, conv_w_half, conv_b_half,
              a_log_pad, seq_len):
    t = u.shape[0]
    tm = ROW_TILE
    row = lambda width: pl.BlockSpec((tm, width), lambda i: (i, 0))
    per_chunk = pl.BlockSpec((tm // SSM_CHUNK, SSM_HEADS, SSM_CHUNK), lambda i: (i, 0, 0))
    widths = (SSM_INNER, CONV_DIM, 2 * D_MODEL)
    out_shape = [jax.ShapeDtypeStruct((t, w), BF16) for w in widths]
    out_shape.append(jax.ShapeDtypeStruct((t, LANES), F32))
    out_shape += [jax.ShapeDtypeStruct((t // SSM_CHUNK, SSM_HEADS, SSM_CHUNK), F32)] * 2
    tri = _chunk_tri_matrix()
    return pl.pallas_call(
        functools.partial(_ssm_proj_kernel, tiles_per_seq=seq_len // tm),
        grid=(t // tm,),
        in_specs=[row(D_MODEL), _const_spec((D_MODEL, SSM_INNER)), _const_spec((D_MODEL, CONV_DIM)),
                  _const_spec((D_MODEL, 2 * D_MODEL)), _const_spec((D_MODEL, LANES)),
                  _const_spec((1, 2 * D_MODEL)), _const_spec((1, LANES)),
                  _const_spec((SSM_CONV, CONV_DIM)), _const_spec((1, CONV_DIM)),
                  _const_spec((1, LANES)), _const_spec(tri.shape)],
        out_specs=[row(w) for w in widths] + [row(LANES), per_chunk, per_chunk],
        out_shape=out_shape,
        scratch_shapes=[pltpu.VMEM((_HALO, CONV_DIM), F32),
                        pltpu.VMEM((_PAD_BUFFERS, 2 * (_HALO + tm), LANES), F32)],
        compiler_params=_params(1),
        name="ssm_proj",
    )(u, w_z_half, w_xbc, w_gate, w_dt_pad, b_gate, dt_bias_pad, conv_w_half, conv_b_half,
      a_log_pad, tri)


ATT_SUB = 16
ATT_SUB16 = 16


def _alibi_slopes(n):
    def pow2(m):
        start = 2.0 ** (-8.0 / m)
        return [start ** (i + 1) for i in range(m)]
    if (n & (n - 1)) == 0:
        s = pow2(n)
    else:
        c = 2 ** int(math.floor(math.log2(n)))
        s = pow2(c) + pow2(2 * c)[0::2][: n - c]
    return np.array(s, dtype=np.float32)


def _attention_bias(dilation, interleave):
    blk = ATT_BLOCK
    seg = blk // interleave
    pos = interleave * (np.arange(blk) % seg) + np.arange(blk) // seg
    i = pos[:, None]
    j = pos[None, :]
    dist_prev = blk + i - j
    dist_cur = i - j
    slopes = (_alibi_slopes(N_ATT_HEADS) * np.float32(_LOG2E)).reshape(N_HEAD_TILES, HEADS_PER_TILE)
    out = np.empty((2, N_HEAD_TILES, HEADS_PER_TILE * blk, 2 * blk), np.float32)
    for tile in range(N_HEAD_TILES):
        for hh in range(HEADS_PER_TILE):
            s = slopes[tile, hh]
            prev = np.where(dist_prev <= blk, -s * (dist_prev * dilation).astype(np.float32), NEG)
            cur = np.where(dist_cur >= 0, -s * (dist_cur * dilation).astype(np.float32), NEG)
            rows = slice(hh * blk, (hh + 1) * blk)
            out[1, tile, rows, :blk] = prev
            out[0, tile, rows, :blk] = NEG
            out[:, tile, rows, blk:] = cur
    return jnp.asarray(out)


DEN_LANE = 16
assert N_ATT_HEADS <= DEN_LANE


def _attention_kernel(cur_ref, prev_ref, bias_ref, o_ref, st_ref, *, dilation):
    blk = ATT_BLOCK
    seg = blk // 4
    n_spans, n_res = cur_ref.shape[0], cur_ref.shape[1]
    n = pl.program_id(1)
    lane = lax.broadcasted_iota(jnp.int32, (blk, LANES), 1)
    in_head_a = lane < HEAD_DIM
    keep_a = jnp.where(in_head_a, 1.0, 0.0).astype(BF16)
    keep_b = jnp.where(in_head_a, 0.0, 1.0).astype(BF16)
    ones = jnp.ones((2 * blk, LANES), BF16)

    def block(ref, sp, g, cols):
        if dilation == 1:
            return jnp.concatenate([ref[sp, r, g * seg:(g + 1) * seg, cols] for r in range(4)], axis=0)
        return ref[sp, g, :, cols]

    def store(ref, sp, g, cols, value):
        if dilation == 1:
            for r in range(4):
                ref[sp, r, g * seg:(g + 1) * seg, cols] = value[r * seg:(r + 1) * seg]
        else:
            ref[sp, g, :, cols] = value

    def previous(sp, g, cols):
        if dilation == 1:
            if g > 0:
                return block(cur_ref, sp, g - 1, cols), None
            if sp > 0:
                return block(cur_ref, sp - 1, n_res - 1, cols), None
            return jnp.concatenate([prev_ref[0, r, :, cols] for r in range(4)], axis=0), n > 0
        if sp > 0:
            return cur_ref[sp - 1, g, :, cols], None
        return prev_ref[0, g, :, cols], n > 0

    for sp, g in [(sp, g) for sp in range(n_spans) for g in range(n_res)]:
        stats = jnp.zeros((blk, LANES), F32)
        for tile in range(N_HEAD_TILES):
            cols = slice(tile * LANES, (tile + 1) * LANES)
            k_cols = cols
            v_cols = slice(ATT_WIDTH + tile * LANES, ATT_WIDTH + (tile + 1) * LANES)
            q_cols = slice(2 * ATT_WIDTH + tile * LANES, 2 * ATT_WIDTH + (tile + 1) * LANES)
            k_prev, has_prev = previous(sp, g, k_cols)
            v_prev, _ = previous(sp, g, v_cols)
            q = block(cur_ref, sp, g, q_cols)
            q2 = jnp.concatenate([q * keep_a, q * keep_b], axis=0)
            k2 = jnp.concatenate([k_prev, block(cur_ref, sp, g, k_cols)], axis=0)
            s = lax.dot_general(q2, k2, (((1,), (1,)), ((), ())), preferred_element_type=F32)
            if has_prev is None:
                s = s + bias_ref[1, tile]
            else:
                s = s + bias_ref[has_prev.astype(jnp.int32), tile]
            m = jnp.max(s, axis=-1, keepdims=True)
            p = jnp.exp2(s - m).astype(BF16)
            v2 = jnp.concatenate([v_prev, block(cur_ref, sp, g, v_cols)], axis=0)
            r = jnp.dot(p, jnp.concatenate([v2, ones], axis=1), preferred_element_type=F32)
            acc, den = r[:, :LANES], r[:, LANES:]
            store(o_ref, sp, g, cols, jnp.where(in_head_a, acc[:blk], acc[blk:]).astype(BF16))
            for hh in range(HEADS_PER_TILE):
                head = tile * HEADS_PER_TILE + hh
                stats = jnp.where(lane == head, m[hh * blk:(hh + 1) * blk], stats)
                stats = jnp.where(lane == DEN_LANE + head, den[hh * blk:(hh + 1) * blk], stats)
        store(st_ref, sp, g, slice(None), stats)


def _attention(qkv, dilation, batch):
    spans, residues, blk, _ = qkv.shape
    per_batch = spans // batch
    d = dilation
    prev_rows = blk
    if d == 16:
        step_spans, step_res = 1, ATT_SUB16
        grid = (batch, per_batch, residues // step_res)
        cur_map = lambda bi, n, rg: (bi * per_batch + n, rg, 0, 0)
        prev_map = lambda bi, n, rg: (bi * per_batch + jnp.maximum(n - 1, 0), rg, 0, 0)
    else:
        step_spans, step_res = ATT_SUB // residues, residues
        steps = per_batch // step_spans
        grid = (batch, steps)
        cur_map = lambda bi, n: (bi * steps + n, 0, 0, 0)
        last_span = lambda bi, n: bi * per_batch + jnp.maximum(step_spans * n - 1, 0)
        if d == 1:
            prev_rows = blk // 4
            prev_map = lambda bi, n: (last_span(bi, n), 0, 3, 0)
        else:
            prev_map = lambda bi, n: (last_span(bi, n), 0, 0, 0)
    cur = lambda w: pl.BlockSpec((step_spans, step_res, blk, w), cur_map)
    prev = lambda w: pl.BlockSpec((1, step_res, prev_rows, w), prev_map)
    bias = _attention_bias(d, 4 if d == 1 else 1)
    return pl.pallas_call(
        functools.partial(_attention_kernel, dilation=d),
        grid=grid,
        in_specs=[cur(3 * ATT_WIDTH), prev(2 * ATT_WIDTH), _const_spec(bias.shape)],
        out_specs=[cur(ATT_WIDTH), cur(LANES)],
        out_shape=[jax.ShapeDtypeStruct((spans, residues, blk, ATT_WIDTH), BF16),
                   jax.ShapeDtypeStruct((spans, residues, blk, LANES), F32)],
        compiler_params=_params(len(grid)),
        name=f"attention_d{d}",
    )(qkv, qkv, bias)


SSD_SUB = 8


def _head_lane_matrix(n_heads, rows):
    e = np.zeros((rows, n_heads * HEAD_DIM), np.float32)
    for h in range(n_heads):
        e[h, h * HEAD_DIM:(h + 1) * HEAD_DIM] = 1.0
    return jnp.asarray(e, dtype=BF16)


def _group_head_masks():
    m = np.zeros((HEADS_PER_GROUP, SSM_CHUNK, GROUP_WIDTH), np.float32)
    for hh in range(HEADS_PER_GROUP):
        m[hh, :, hh * HEAD_DIM:(hh + 1) * HEAD_DIM] = 1.0
    return jnp.asarray(m, dtype=BF16)


def _ssd_kernel(xc_ref, zs_ref, acs_ref, asrc_ref, wt_ref, spread_ref, mask_ref, dskip_ref, nw_ref,
                y_ref, st_ref):
    @pl.when(pl.program_id(1) == 0)
    def _():
        st_ref[...] = jnp.zeros_like(st_ref)

    for sub in range(SSD_SUB):
        _ssd_chunk(sub, xc_ref, zs_ref, acs_ref, asrc_ref, wt_ref, spread_ref, mask_ref, dskip_ref,
                   nw_ref, y_ref, st_ref)


def _ssd_chunk(sub, xc_ref, zs_ref, acs_ref, asrc_ref, wt_ref, spread_ref, mask_ref, dskip_ref,
               nw_ref, y_ref, st_ref):
    ch = SSM_CHUNK
    rows = slice(sub * ch, (sub + 1) * ch)
    a_cs = acs_ref[0, rows, :]
    a_src_t = asrc_ref[0, sub]
    w_bf = wt_ref[0, sub].astype(BF16)
    row_i = lax.broadcasted_iota(jnp.int32, (ch, ch), 0)
    col_i = lax.broadcasted_iota(jnp.int32, (ch, ch), 1)
    causal = row_i >= col_i
    e = jnp.exp2(a_cs)
    a_exp_all = jnp.dot(e.astype(BF16), spread_ref[...], preferred_element_type=F32)
    e_end = jnp.broadcast_to(e[ch - 1:ch, :], (8, LANES))
    e_end_hi = e_end.astype(BF16)
    e_end_lo = (e_end - e_end_hi.astype(F32)).astype(BF16)
    chunk_decay = (jnp.dot(e_end_hi, spread_ref[...], preferred_element_type=F32)
                   + jnp.dot(e_end_lo, spread_ref[...], preferred_element_type=F32))[0:1]

    for g in range(SSM_GROUPS):
        gs = slice(g * GROUP_WIDTH, (g + 1) * GROUP_WIDTH)
        b_bf = xc_ref[0, rows, SSM_INNER + g * SSM_STATE:SSM_INNER + (g + 1) * SSM_STATE]
        c_bf = xc_ref[0, rows, SSM_INNER + (SSM_GROUPS + g) * SSM_STATE:
                      SSM_INNER + (SSM_GROUPS + g + 1) * SSM_STATE]
        cb = lax.dot_general(c_bf, b_bf, (((1,), (1,)), ((), ())),
                             preferred_element_type=F32).astype(BF16)
        b_t = b_bf.astype(F32).T.astype(BF16)
        x_bf = xc_ref[0, rows, gs]

        intra, to_state, x_diag = [], [], []
        for hh in range(HEADS_PER_GROUP):
            h = g * HEADS_PER_GROUP + hh
            a_col = jnp.broadcast_to(a_cs[:, h:h + 1], (ch, ch))
            decay_dt = jnp.exp2(jnp.where(causal, a_col - a_src_t[h:h + 1, :], NEG))
            intra.append(cb * decay_dt.astype(BF16))
            to_state.append(b_t * w_bf[h:h + 1, :])
            x_diag.append(x_bf * mask_ref[hh])
        lhs = jnp.concatenate([jnp.concatenate(intra, axis=1), jnp.concatenate(to_state, axis=1)], axis=0)
        r = jnp.dot(lhs, jnp.concatenate(x_diag, axis=0), preferred_element_type=F32)
        y_diag, st_inc = r[:ch], r[ch:]

        a_exp = a_exp_all[:, gs]
        st_old = st_ref[g]
        y_off = jnp.dot(c_bf, st_old.astype(BF16), preferred_element_type=F32) * a_exp
        st_ref[g] = st_old * chunk_decay[:, gs] + st_inc

        y = y_diag + y_off + dskip_ref[:, gs] * x_bf.astype(F32)
        y = y * zs_ref[0, rows, gs].astype(F32)
        y_ref[0, rows, gs] = (y * _rms_scale(y) * nw_ref[:, gs]).astype(BF16)


def _ssd(xc, zs, a_cs, a_src_t, w_t, d_skip_wide, norm_w):
    b, s, _ = xc.shape
    ch = SSM_CHUNK * SSD_SUB
    blk = lambda w: pl.BlockSpec((1, ch, w), lambda bi, c: (bi, c, 0))
    per_chunk = pl.BlockSpec((1, SSD_SUB, SSM_HEADS, SSM_CHUNK), lambda bi, c: (bi, c, 0, 0))
    spread = _head_lane_matrix(SSM_HEADS, LANES)
    masks = _group_head_masks()
    return pl.pallas_call(
        _ssd_kernel,
        grid=(b, s // ch),
        in_specs=[blk(CONV_DIM), blk(SSM_INNER), blk(LANES), per_chunk, per_chunk,
                  _const_spec(spread.shape), _const_spec(masks.shape),
                  _const_spec((1, SSM_INNER)), _const_spec((1, SSM_INNER))],
        out_specs=blk(SSM_INNER),
        out_shape=jax.ShapeDtypeStruct((b, s, SSM_INNER), BF16),
        scratch_shapes=[pltpu.VMEM((SSM_GROUPS, SSM_STATE, GROUP_WIDTH), F32)],
        compiler_params=_params(2),
        name="ssd",
    )(xc, zs, a_cs, a_src_t, w_t, spread, masks, d_skip_wide, norm_w)


_ATT_SLABS = ATT_WIDTH // LANES


def _merge_kernel(x_ref, o1_ref, o4_ref, o16_ref, s1_ref, s4_ref, s16_ref, y_ref, gate_ref,
                  expand_ref, watt_ref, wssm_ref, wout_ref, nw_ref, pre_ref, h_ref, f_ref,
                  onat_ref, snat_ref, tmp_ref):
    tm = ROW_TILE
    ssm = jnp.dot(y_ref[...], wssm_ref[...], preferred_element_type=F32)
    for slot, (s_ref, o_ref, d) in enumerate(((s1_ref, o1_ref, 4), (s4_ref, o4_ref, 4),
                                              (s16_ref, o16_ref, 16))):
        for r in range(d):
            snat_ref[slot, pl.ds(r, tm // d, stride=d), :] = s_ref[0, r]
        for slab in range(_ATT_SLABS):
            cols = slice(slab * LANES, (slab + 1) * LANES)
            if d == 4:
                for r in range(d):
                    onat_ref[slot, slab, pl.ds(r, tm // d, stride=d), :] = o_ref[0, r, :, cols].astype(F32)
            else:
                for r in range(d):
                    tmp_ref[slab, r % 4, pl.ds(r // 4, tm // d, stride=4), :] = (
                        o_ref[0, r, :, cols].astype(F32))
                for r4 in range(4):
                    onat_ref[slot, slab, pl.ds(r4, tm // 4, stride=4), :] = tmp_ref[slab, r4]

    stats = [snat_ref[0], snat_ref[1], snat_ref[2]]
    top = jnp.maximum(jnp.maximum(stats[0], stats[1]), stats[2])
    scale = [jnp.exp2(st - top) for st in stats]
    dens = [pltpu.roll(st, LANES - DEN_LANE, axis=1) for st in stats]
    total = scale[0] * dens[0] + scale[1] * dens[1] + scale[2] * dens[2]
    is_head = lax.broadcasted_iota(jnp.int32, (tm, LANES), 1) < N_ATT_HEADS
    inv = 1.0 / jnp.where(is_head, total, 1.0)
    wts = [jnp.where(is_head, sc * inv, 0.0).astype(BF16) for sc in scale]
    slabs = []
    for slab in range(_ATT_SLABS):
        cols = slice(slab * LANES, (slab + 1) * LANES)
        wide = [jnp.dot(w, expand_ref[:, cols], preferred_element_type=F32) for w in wts]
        slabs.append(wide[0] * onat_ref[0, slab] + wide[1] * onat_ref[1, slab]
                     + wide[2] * onat_ref[2, slab])
    att = jnp.concatenate(slabs, axis=1).astype(BF16)
    att = jnp.dot(att, watt_ref[...], preferred_element_type=F32)
    gates = gate_ref[...].astype(F32)
    mixed = gates[:, :D_MODEL] * att + gates[:, D_MODEL:] * ssm
    mixed = jnp.dot(mixed.astype(BF16), wout_ref[...], preferred_element_type=F32)
    h = x_ref[...] + mixed * _rms_scale(mixed) * nw_ref[...]
    h_ref[...] = h
    f_ref[...] = (h * _rms_scale(h) * pre_ref[...]).astype(BF16)


def _merge(x2, outs, stats, y, gates, expand, w_att, w_ssm, w_out, norm_w, ffn_pre_w):
    t = x2.shape[0]
    tm = ROW_TILE
    row = lambda width: pl.BlockSpec((tm, width), lambda i: (i, 0))
    return pl.pallas_call(
        _merge_kernel,
        grid=(t // tm,),
        in_specs=[row(D_MODEL),
                  _residue4_spec(ATT_WIDTH), _residue4_spec(ATT_WIDTH), _residue16_spec(ATT_WIDTH),
                  _residue4_spec(LANES), _residue4_spec(LANES), _residue16_spec(LANES),
                  row(SSM_INNER), row(2 * D_MODEL),
                  _const_spec((LANES, ATT_WIDTH)), _const_spec((ATT_WIDTH, D_MODEL)),
                  _const_spec((SSM_INNER, D_MODEL)), _const_spec((D_MODEL, D_MODEL)),
                  _const_spec((1, D_MODEL)), _const_spec((1, D_MODEL))],
        out_specs=[row(D_MODEL), row(D_MODEL)],
        out_shape=[jax.ShapeDtypeStruct((t, D_MODEL), F32), jax.ShapeDtypeStruct((t, D_MODEL), BF16)],
        scratch_shapes=[pltpu.VMEM((3, _ATT_SLABS, tm, LANES), F32),
                        pltpu.VMEM((3, tm, LANES), F32),
                        pltpu.VMEM((_ATT_SLABS, 4, tm // 4, LANES), F32)],
        compiler_params=_params(1),
        name="merge",
    )(x2, *outs, *stats, y, gates, expand, w_att, w_ssm, w_out, norm_w, ffn_pre_w)


FFN_CHUNK = 1024
FFN_TILE = 1024


def _ffn_kernel(h_ref, f_ref, wup_ref, wdown_ref, post_ref, o_ref):
    acc = jnp.zeros((FFN_TILE, D_MODEL), F32)
    for c in range(0, FFN_HIDDEN, FFN_CHUNK):
        up = jnp.dot(f_ref[...], wup_ref[:, c:c + FFN_CHUNK], preferred_element_type=F32)
        act = jnp.square(jnp.maximum(up, 0.0)).astype(BF16)
        acc = acc + jnp.dot(act, wdown_ref[c:c + FFN_CHUNK, :], preferred_element_type=F32)
    o_ref[...] = h_ref[...] + acc * _rms_scale(acc) * post_ref[...]


def _ffn(h2, f2, w_up, w_down, post_w):
    t = h2.shape[0]
    row = pl.BlockSpec((FFN_TILE, D_MODEL), lambda i: (i, 0))
    return pl.pallas_call(
        _ffn_kernel,
        grid=(t // FFN_TILE,),
        in_specs=[row, row, _const_spec((D_MODEL, FFN_HIDDEN)),
                  _const_spec((FFN_HIDDEN, D_MODEL)), _const_spec((1, D_MODEL))],
        out_specs=row,
        out_shape=jax.ShapeDtypeStruct((t, D_MODEL), F32),
        compiler_params=_params(1),
        name="ffn",
    )(h2, f2, w_up, w_down, post_w)


def _pad_lanes(v):
    return jnp.pad(v.astype(F32), (0, LANES - v.shape[0])).reshape(1, LANES)


PREP_ROWS = 256


def _transpose_cast_kernel(wt_ref, o_ref, *, scale, first, total_rows):
    row = (first + pl.program_id(0)) * PREP_ROWS + lax.broadcasted_iota(jnp.int32, wt_ref.shape, 0)
    w = jnp.where(row < total_rows, wt_ref[...], 0.0)
    o_ref[...] = (w.T * scale).astype(BF16)


def _transpose_cast(wt, row0, n_rows, scale=1.0):
    k = wt.shape[1]
    first = row0 // PREP_ROWS
    assert row0 % PREP_ROWS == 0 and n_rows % PREP_ROWS == 0
    return pl.pallas_call(
        functools.partial(_transpose_cast_kernel, scale=scale, first=first, total_rows=wt.shape[0]),
        grid=(n_rows // PREP_ROWS,),
        in_specs=[pl.BlockSpec((PREP_ROWS, k), lambda j: (first + j, 0))],
        out_specs=pl.BlockSpec((k, PREP_ROWS), lambda j: (0, j)),
        out_shape=jax.ShapeDtypeStruct((k, n_rows), BF16),
        compiler_params=_params(1),
        name="weight_prep",
    )(wt)


def kernel(x, norm_mix_pre_w, w_in, b_gate, conv_w, conv_b, dt_bias, a_log, d_skip, ssm_norm_w,
           w_att_proj, w_ssm_proj, w_out, norm_mix_post_w, norm_ffn_pre_w, w_up, w_down,
           norm_ffn_post_w):
    b, s, d_model = x.shape
    depth = w_in.shape[0]
    assert d_model == D_MODEL and s % SPAN16 == 0
    t = b * s
    expand = _head_lane_matrix(N_ATT_HEADS, LANES)
    h = x.reshape(t, D_MODEL)
    for layer in range(depth):
        w_t_in = jnp.transpose(w_in[layer])
        off_ssm = 3 * ATT_WIDTH
        off_xbc = off_ssm + SSM_INNER
        off_dt = off_xbc + CONV_DIM
        w_qkv = _transpose_cast(w_t_in, 0, off_ssm)
        w_z_half = _transpose_cast(w_t_in, off_ssm, SSM_INNER, 0.5)
        w_xbc = _transpose_cast(w_t_in, off_xbc, CONV_DIM)
        n_tail = 2 * D_MODEL + SSM_HEADS
        w_tail = _transpose_cast(w_t_in, off_dt, -(-n_tail // PREP_ROWS) * PREP_ROWS)
        w_gate_half = 0.5 * w_tail[:, SSM_HEADS:n_tail]
        w_dt_pad = jnp.where(jnp.arange(LANES) < SSM_HEADS, w_tail[:, :LANES], 0).astype(BF16)
        qkv4, qkv16, u = _qkv_proj(h, norm_mix_pre_w[layer].reshape(1, -1), w_qkv)
        zs, xc, gates, a_cs, a_src_t, w_t = _ssm_proj(
            u, w_z_half, w_xbc, w_gate_half, w_dt_pad, 0.5 * b_gate[layer].reshape(1, -1),
            _pad_lanes(dt_bias[layer]),
            0.5 * conv_w[layer], 0.5 * conv_b[layer].reshape(1, -1), _pad_lanes(a_log[layer]), s)
        outs, stats = [], []
        for d in DILATIONS:
            o, st = _attention(qkv16 if d == 16 else qkv4, d, b)
            outs.append(o)
            stats.append(st)
        seq = lambda a: a.reshape(b, s, a.shape[-1])
        per_chunk = lambda a: a.reshape(b, s // SSM_CHUNK, SSM_HEADS, SSM_CHUNK)
        y = _ssd(seq(xc), seq(zs), seq(a_cs), per_chunk(a_src_t), per_chunk(w_t),
                 jnp.repeat(d_skip[layer].astype(F32), HEAD_DIM).reshape(1, -1),
                 ssm_norm_w[layer].reshape(1, -1))
        h, f = _merge(h, outs, stats, y.reshape(t, SSM_INNER), gates, expand,
                      w_att_proj[layer].astype(BF16), w_ssm_proj[layer].astype(BF16),
                      w_out[layer].astype(BF16), norm_mix_post_w[layer].reshape(1, -1),
                      norm_ffn_pre_w[layer].reshape(1, -1))
        h = _ffn(h, f, w_up[layer].astype(BF16), w_down[layer].astype(BF16),
                 norm_ffn_post_w[layer].reshape(1, -1))
    return h.reshape(b, s, D_MODEL)
```

```python
import functools
import math

import numpy as np
import jax
import jax.numpy as jnp
from jax import lax
from jax.experimental import pallas as pl
from jax.experimental.pallas import tpu as pltpu

F32 = jnp.float32
BF16 = jnp.bfloat16

D_MODEL = 1024
HEAD_DIM = 64
N_ATT_HEADS = 12
ATT_WIDTH = N_ATT_HEADS * HEAD_DIM
DILATIONS = (1, 4, 16)
ATT_BLOCK = 128
SSM_INNER = 2048
SSM_HEADS = 32
SSM_GROUPS = 8
SSM_STATE = 128
SSM_CONV = 4
SSM_CHUNK = 128
CONV_DIM = SSM_INNER + 2 * SSM_GROUPS * SSM_STATE
FFN_HIDDEN = 4 * D_MODEL
RMS_EPS = 1e-6
NEG = -1e30

LANES = 128
MXU_N = 256
HEADS_PER_TILE = LANES // HEAD_DIM
N_HEAD_TILES = N_ATT_HEADS // HEADS_PER_TILE
GROUP_WIDTH = SSM_INNER // SSM_GROUPS
HEADS_PER_GROUP = SSM_HEADS // SSM_GROUPS
VMEM_LIMIT_BYTES = 56 * 1024 * 1024

ROW_TILE = 512
SPAN4 = ATT_BLOCK * 4
SPAN16 = ATT_BLOCK * 16
assert ROW_TILE == SPAN4


def _const_spec(shape):
    return pl.BlockSpec(shape, lambda *_: (0,) * len(shape), pipeline_mode=pl.Buffered(1))


def _params(n_axes):
    return pltpu.CompilerParams(dimension_semantics=("arbitrary",) * n_axes,
                                vmem_limit_bytes=VMEM_LIMIT_BYTES)


def _rms_scale(x):
    return lax.rsqrt(jnp.mean(x * x, axis=-1, keepdims=True) + RMS_EPS)


def _residue16_spec(width):
    per_span = SPAN16 // ROW_TILE
    return pl.BlockSpec((1, 16, ROW_TILE // 16, width), lambda i: (i // per_span, 0, i % per_span, 0))


def _residue4_spec(width):
    return pl.BlockSpec((1, 4, ATT_BLOCK, width), lambda i: (i, 0, 0, 0))


_QKV_SLABS = 3 * ATT_WIDTH // LANES
_LOG2E = 1.4426950408889634
Q_SCALE = HEAD_DIM ** -0.5 * _LOG2E


def _qkv_kernel(x_ref, nw_ref, w_ref, *refs):
    d4_ref, d16_ref, u_ref, slab_ref, res4_ref = refs
    tm = ROW_TILE
    x = x_ref[...]
    u_ref[...] = (x * _rms_scale(x) * nw_ref[...]).astype(BF16)
    for a in range(3):
        for c in range(0, ATT_WIDTH, MXU_N):
            acc = jnp.dot(u_ref[...], w_ref[:, a * ATT_WIDTH + c:a * ATT_WIDTH + c + MXU_N],
                          preferred_element_type=F32)
            if a == 0:
                acc = acc * Q_SCALE
            for half in range(MXU_N // LANES):
                slab = (a * ATT_WIDTH + c) // LANES + half
                lane0 = ((a + 2) % 3) * ATT_WIDTH + c + half * LANES
                cols = slice(lane0, lane0 + LANES)
                slab_ref[slab] = acc[:, half * LANES:(half + 1) * LANES]
                for r in range(4):
                    part = slab_ref[slab, pl.ds(r, tm // 4, stride=4), :]
                    d4_ref[0, r, :, cols] = part.astype(BF16)
                    res4_ref[slab, r] = part
                for r in range(16):
                    part = res4_ref[slab, r % 4, pl.ds(r // 4, tm // 16, stride=4), :]
                    d16_ref[0, r, :, cols] = part.astype(BF16)


def _qkv_proj(x2, norm_w, w_qkv):
    t = x2.shape[0]
    tm = ROW_TILE
    shapes = [jax.ShapeDtypeStruct((t // SPAN4, 4, ATT_BLOCK, 3 * ATT_WIDTH), BF16),
              jax.ShapeDtypeStruct((t // SPAN16, 16, ATT_BLOCK, 3 * ATT_WIDTH), BF16)]
    return pl.pallas_call(
        _qkv_kernel,
        grid=(t // tm,),
        in_specs=[pl.BlockSpec((tm, D_MODEL), lambda i: (i, 0)), _const_spec((1, D_MODEL)),
                  _const_spec((D_MODEL, 3 * ATT_WIDTH))],
        out_specs=[_residue4_spec(3 * ATT_WIDTH), _residue16_spec(3 * ATT_WIDTH),
                   pl.BlockSpec((tm, D_MODEL), lambda i: (i, 0))],
        out_shape=shapes + [jax.ShapeDtypeStruct((t, D_MODEL), BF16)],
        scratch_shapes=[pltpu.VMEM((_QKV_SLABS, tm, LANES), F32),
                        pltpu.VMEM((_QKV_SLABS, 4, tm // 4, LANES), F32)],
        compiler_params=_params(1),
        name="qkv_proj",
    )(x2, norm_w, w_qkv)


_HALO = 8
_PAD_BUFFERS = 8
PROJ_N = 1024


def _split3(x):
    hi = x.astype(BF16)
    r = x - hi.astype(F32)
    mid = r.astype(BF16)
    lo = (r - mid.astype(F32)).astype(BF16)
    return hi, mid, lo


def _chunk_tri_matrix():
    i = np.arange(ROW_TILE)
    same_chunk = (i[:, None] // SSM_CHUNK) == (i[None, :] // SSM_CHUNK)
    return jnp.asarray(same_chunk & (i[:, None] >= i[None, :]), dtype=BF16)


def _ssm_proj_kernel(u_ref, wz_ref, wx_ref, wg_ref, wdt_ref, bg_ref, dtb_ref, cw_ref, cb_ref,
                     alog_ref, tri_ref, zs_ref, xc_ref, gate_ref, acs_ref, asrc_ref, wt_ref,
                     halo_ref, pad_ref, *, tiles_per_seq):
    tm = ROW_TILE

    @pl.when(pl.program_id(0) % tiles_per_seq == 0)
    def _():
        halo_ref[...] = jnp.zeros_like(halo_ref)

    def proj(w_ref, off, width):
        return jnp.dot(u_ref[...], w_ref[:, off:off + width], preferred_element_type=F32)

    dt_raw = proj(wdt_ref, 0, LANES) + dtb_ref[...]
    dt = jnp.maximum(dt_raw, 0.0) + jnp.log1p(jnp.exp(-jnp.abs(dt_raw)))
    la = dt * (-_LOG2E * jnp.exp(alog_ref[...]))
    a_cs = sum(jnp.dot(tri_ref[...], part, preferred_element_type=F32) for part in _split3(la))
    acs_ref[...] = a_cs
    ch = SSM_CHUNK
    for cidx in range(tm // ch):
        a_t = a_cs[cidx * ch:(cidx + 1) * ch].T
        dt_t = dt[cidx * ch:(cidx + 1) * ch].T
        asrc_ref[cidx] = (a_t - jnp.log2(dt_t))[:SSM_HEADS]
        wt_ref[cidx] = (jnp.exp2(a_t[:, ch - 1:ch] - a_t) * dt_t)[:SSM_HEADS]

    def z_chunk(c):
        h = proj(wz_ref, c, PROJ_N)
        zs_ref[:, c:c + PROJ_N] = (h + h * jnp.tanh(h)).astype(BF16)

    def gate_chunk(c):
        h = proj(wg_ref, c, PROJ_N) + bg_ref[:, c:c + PROJ_N]
        gate_ref[:, c:c + PROJ_N] = (0.5 + 0.5 * jnp.tanh(h)).astype(BF16)

    def conv_chunk(idx):
        acc2 = proj(wx_ref, idx * PROJ_N, PROJ_N)
        for part in range(PROJ_N // LANES):
            c = idx * PROJ_N + part * LANES
            cs = slice(c, c + LANES)
            buf = (c // LANES) % _PAD_BUFFERS
            acc = acc2[:, part * LANES:(part + 1) * LANES]
            pad_ref[buf, pl.ds(0, _HALO, stride=2), :] = halo_ref[:, cs]
            pad_ref[buf, pl.ds(2 * _HALO, tm, stride=2), :] = acc
            halo_ref[:, cs] = acc[tm - _HALO:, :]
            cw = cw_ref[:, cs].astype(BF16)
            h = cb_ref[:, cs].astype(BF16) + cw[SSM_CONV - 1:SSM_CONV] * acc.astype(BF16)
            for tap in range(SSM_CONV - 1):
                start = 2 * (_HALO - (SSM_CONV - 1) + tap)
                h = h + cw[tap:tap + 1] * pad_ref[buf, pl.ds(start, tm, stride=2), :].astype(BF16)
            xc_ref[:, cs] = h + h * jnp.tanh(h)

    light = ([functools.partial(z_chunk, c) for c in range(0, SSM_INNER, PROJ_N)]
             + [functools.partial(gate_chunk, c) for c in range(0, 2 * D_MODEL, PROJ_N)])
    for idx in range(CONV_DIM // PROJ_N):
        conv_chunk(idx)
        light[idx]()


def _ssm_proj(u, w_z_half, w_xbc, w_gate, w_dt_pad, b_gate, dt_bias_pad, conv_w_half, conv_b_half,
              a_log_pad, seq_len):
    t = u.shape[0]
    tm = ROW_TILE
    row = lambda width: pl.BlockSpec((tm, width), lambda i: (i, 0))
    per_chunk = pl.BlockSpec((tm // SSM_CHUNK, SSM_HEADS, SSM_CHUNK), lambda i: (i, 0, 0))
    widths = (SSM_INNER, CONV_DIM, 2 * D_MODEL)
    out_shape = [jax.ShapeDtypeStruct((t, w), BF16) for w in widths]
    out_shape.append(jax.ShapeDtypeStruct((t, LANES), F32))
    out_shape += [jax.ShapeDtypeStruct((t // SSM_CHUNK, SSM_HEADS, SSM_CHUNK), F32)] * 2
    tri = _chunk_tri_matrix()
    return pl.pallas_call(
        functools.partial(_ssm_proj_kernel, tiles_per_seq=seq_len // tm),
        grid=(t // tm,),
        in_specs=[row(D_MODEL), _const_spec((D_MODEL, SSM_INNER)), _const_spec((D_MODEL, CONV_DIM)),
                  _const_spec((D_MODEL, 2 * D_MODEL)), _const_spec((D_MODEL, LANES)),
                  _const_spec((1, 2 * D_MODEL)), _const_spec((1, LANES)),
                  _const_spec((SSM_CONV, CONV_DIM)), _const_spec((1, CONV_DIM)),
                  _const_spec((1, LANES)), _const_spec(tri.shape)],
        out_specs=[row(w) for w in widths] + [row(LANES), per_chunk, per_chunk],
        out_shape=out_shape,
        scratch_shapes=[pltpu.VMEM((_HALO, CONV_DIM), F32),
                        pltpu.VMEM((_PAD_BUFFERS, 2 * (_HALO + tm), LANES), F32)],
        compiler_params=_params(1),
        name="ssm_proj",
    )(u, w_z_half, w_xbc, w_gate, w_dt_pad, b_gate, dt_bias_pad, conv_w_half, conv_b_half,
      a_log_pad, tri)


ATT_SUB = 16
ATT_SUB16 = 16


def _alibi_slopes(n):
    def pow2(m):
        start = 2.0 ** (-8.0 / m)
        return [start ** (i + 1) for i in range(m)]
    if (n & (n - 1)) == 0:
        s = pow2(n)
    else:
        c = 2 ** int(math.floor(math.log2(n)))
        s = pow2(c) + pow2(2 * c)[0::2][: n - c]
    return np.array(s, dtype=np.float32)


def _attention_bias(dilation, interleave):
    blk = ATT_BLOCK
    seg = blk // interleave
    pos = interleave * (np.arange(blk) % seg) + np.arange(blk) // seg
    i = pos[:, None]
    j = pos[None, :]
    dist_prev = blk + i - j
    dist_cur = i - j
    slopes = (_alibi_slopes(N_ATT_HEADS) * np.float32(_LOG2E)).reshape(N_HEAD_TILES, HEADS_PER_TILE)
    out = np.empty((2, N_HEAD_TILES, HEADS_PER_TILE * blk, 2 * blk), np.float32)
    for tile in range(N_HEAD_TILES):
        for hh in range(HEADS_PER_TILE):
            s = slopes[tile, hh]
            prev = np.where(dist_prev <= blk, -s * (dist_prev * dilation).astype(np.float32), NEG)
            cur = np.where(dist_cur >= 0, -s * (dist_cur * dilation).astype(np.float32), NEG)
            rows = slice(hh * blk, (hh + 1) * blk)
            out[1, tile, rows, :blk] = prev
            out[0, tile, rows, :blk] = NEG
            out[:, tile, rows, blk:] = cur
    return jnp.asarray(out)


DEN_LANE = 16
assert N_ATT_HEADS <= DEN_LANE


def _attention_kernel(cur_ref, prev_ref, bias_ref, o_ref, st_ref, *, dilation):
    _attention_body(cur_ref, prev_ref, bias_ref, o_ref, st_ref, dilation)


def _attention14_kernel(cur_ref, prev_ref, bias1_ref, bias4_ref, o1_ref, s1_ref, o4_ref, s4_ref):
    _attention_body(cur_ref, prev_ref, bias1_ref, o1_ref, s1_ref, 1,
                    prev_row0=ATT_BLOCK - ATT_BLOCK // 4)
    _attention_body(cur_ref, prev_ref, bias4_ref, o4_ref, s4_ref, 4)


def _attention_body(cur_ref, prev_ref, bias_ref, o_ref, st_ref, dilation, prev_row0=0):
    blk = ATT_BLOCK
    seg = blk // 4
    n_spans, n_res = cur_ref.shape[0], cur_ref.shape[1]
    n = pl.program_id(1)
    lane = lax.broadcasted_iota(jnp.int32, (blk, LANES), 1)
    in_head_a = lane < HEAD_DIM
    keep_a = jnp.where(in_head_a, 1.0, 0.0).astype(BF16)
    keep_b = jnp.where(in_head_a, 0.0, 1.0).astype(BF16)
    ones = jnp.ones((2 * blk, LANES), BF16)

    def block(ref, sp, g, cols):
        if dilation == 1:
            return jnp.concatenate([ref[sp, r, g * seg:(g + 1) * seg, cols] for r in range(4)], axis=0)
        return ref[sp, g, :, cols]

    def store(ref, sp, g, cols, value):
        if dilation == 1:
            for r in range(4):
                ref[sp, r, g * seg:(g + 1) * seg, cols] = value[r * seg:(r + 1) * seg]
        else:
            ref[sp, g, :, cols] = value

    def previous(sp, g, cols):
        if dilation == 1:
            if g > 0:
                return block(cur_ref, sp, g - 1, cols), None
            if sp > 0:
                return block(cur_ref, sp - 1, n_res - 1, cols), None
            return jnp.concatenate([prev_ref[0, r, prev_row0:prev_row0 + seg, cols]
                                    for r in range(4)], axis=0), n > 0
        if sp > 0:
            return cur_ref[sp - 1, g, :, cols], None
        return prev_ref[0, g, :, cols], n > 0

    for sp, g in [(sp, g) for sp in range(n_spans) for g in range(n_res)]:
        stats = jnp.zeros((blk, LANES), F32)
        for tile in range(N_HEAD_TILES):
            cols = slice(tile * LANES, (tile + 1) * LANES)
            k_cols = cols
            v_cols = slice(ATT_WIDTH + tile * LANES, ATT_WIDTH + (tile + 1) * LANES)
            q_cols = slice(2 * ATT_WIDTH + tile * LANES, 2 * ATT_WIDTH + (tile + 1) * LANES)
            k_prev, has_prev = previous(sp, g, k_cols)
            v_prev, _ = previous(sp, g, v_cols)
            q = block(cur_ref, sp, g, q_cols)
            q2 = jnp.concatenate([q * keep_a, q * keep_b], axis=0)
            k2 = jnp.concatenate([k_prev, block(cur_ref, sp, g, k_cols)], axis=0)
            s = lax.dot_general(q2, k2, (((1,), (1,)), ((), ())), preferred_element_type=F32)
            if has_prev is None:
                s = s + bias_ref[1, tile]
            else:
                s = s + bias_ref[has_prev.astype(jnp.int32), tile]
            m = jnp.max(s, axis=-1, keepdims=True)
            p = jnp.exp2(s - m).astype(BF16)
            v2 = jnp.concatenate([v_prev, block(cur_ref, sp, g, v_cols)], axis=0)
            r = jnp.dot(p, jnp.concatenate([v2, ones], axis=1), preferred_element_type=F32)
            acc, den = r[:, :LANES], r[:, LANES:]
            store(o_ref, sp, g, cols, jnp.where(in_head_a, acc[:blk], acc[blk:]).astype(BF16))
            for hh in range(HEADS_PER_TILE):
                head = tile * HEADS_PER_TILE + hh
                stats = jnp.where(lane == head, m[hh * blk:(hh + 1) * blk], stats)
                stats = jnp.where(lane == DEN_LANE + head, den[hh * blk:(hh + 1) * blk], stats)
        store(st_ref, sp, g, slice(None), stats)


def _attention14(qkv, batch):
    spans, residues, blk, _ = qkv.shape
    per_batch = spans // batch
    step_spans = ATT_SUB // residues
    steps = per_batch // step_spans
    cur_map = lambda bi, n: (bi * steps + n, 0, 0, 0)
    prev_map = lambda bi, n: (bi * per_batch + jnp.maximum(step_spans * n - 1, 0), 0, 0, 0)
    cur = lambda w: pl.BlockSpec((step_spans, residues, blk, w), cur_map)
    bias1, bias4 = _attention_bias(1, 4), _attention_bias(4, 1)
    o_shape = jax.ShapeDtypeStruct((spans, residues, blk, ATT_WIDTH), BF16)
    s_shape = jax.ShapeDtypeStruct((spans, residues, blk, LANES), F32)
    return pl.pallas_call(
        _attention14_kernel,
        grid=(batch, steps),
        in_specs=[cur(3 * ATT_WIDTH), pl.BlockSpec((1, residues, blk, 2 * ATT_WIDTH), prev_map),
                  _const_spec(bias1.shape), _const_spec(bias4.shape)],
        out_specs=[cur(ATT_WIDTH), cur(LANES), cur(ATT_WIDTH), cur(LANES)],
        out_shape=[o_shape, s_shape, o_shape, s_shape],
        compiler_params=_params(2),
        name="attention_d1_d4",
    )(qkv, qkv, bias1, bias4)


def _attention(qkv, dilation, batch):
    spans, residues, blk, _ = qkv.shape
    per_batch = spans // batch
    d = dilation
    prev_rows = blk
    if d == 16:
        step_spans, step_res = 1, ATT_SUB16
        grid = (batch, per_batch, residues // step_res)
        cur_map = lambda bi, n, rg: (bi * per_batch + n, rg, 0, 0)
        prev_map = lambda bi, n, rg: (bi * per_batch + jnp.maximum(n - 1, 0), rg, 0, 0)
    else:
        step_spans, step_res = ATT_SUB // residues, residues
        steps = per_batch // step_spans
        grid = (batch, steps)
        cur_map = lambda bi, n: (bi * steps + n, 0, 0, 0)
        last_span = lambda bi, n: bi * per_batch + jnp.maximum(step_spans * n - 1, 0)
        if d == 1:
            prev_rows = blk // 4
            prev_map = lambda bi, n: (last_span(bi, n), 0, 3, 0)
        else:
            prev_map = lambda bi, n: (last_span(bi, n), 0, 0, 0)
    cur = lambda w: pl.BlockSpec((step_spans, step_res, blk, w), cur_map)
    prev = lambda w: pl.BlockSpec((1, step_res, prev_rows, w), prev_map)
    bias = _attention_bias(d, 4 if d == 1 else 1)
    return pl.pallas_call(
        functools.partial(_attention_kernel, dilation=d),
        grid=grid,
        in_specs=[cur(3 * ATT_WIDTH), prev(2 * ATT_WIDTH), _const_spec(bias.shape)],
        out_specs=[cur(ATT_WIDTH), cur(LANES)],
        out_shape=[jax.ShapeDtypeStruct((spans, residues, blk, ATT_WIDTH), BF16),
                   jax.ShapeDtypeStruct((spans, residues, blk, LANES), F32)],
        compiler_params=_params(len(grid)),
        name=f"attention_d{d}",
    )(qkv, qkv, bias)


SSD_SUB = 8


def _head_lane_matrix(n_heads, rows):
    e = np.zeros((rows, n_heads * HEAD_DIM), np.float32)
    for h in range(n_heads):
        e[h, h * HEAD_DIM:(h + 1) * HEAD_DIM] = 1.0
    return jnp.asarray(e, dtype=BF16)


def _group_head_masks():
    m = np.zeros((HEADS_PER_GROUP, SSM_CHUNK, GROUP_WIDTH), np.float32)
    for hh in range(HEADS_PER_GROUP):
        m[hh, :, hh * HEAD_DIM:(hh + 1) * HEAD_DIM] = 1.0
    return jnp.asarray(m, dtype=BF16)


def _ssd_kernel(xc_ref, zs_ref, acs_ref, asrc_ref, wt_ref, spread_ref, mask_ref, dskip_ref, nw_ref,
                y_ref, st_ref):
    @pl.when(pl.program_id(1) == 0)
    def _():
        st_ref[...] = jnp.zeros_like(st_ref)

    for sub in range(SSD_SUB):
        _ssd_chunk(sub, xc_ref, zs_ref, acs_ref, asrc_ref, wt_ref, spread_ref, mask_ref, dskip_ref,
                   nw_ref, y_ref, st_ref)


def _ssd_chunk(sub, xc_ref, zs_ref, acs_ref, asrc_ref, wt_ref, spread_ref, mask_ref, dskip_ref,
               nw_ref, y_ref, st_ref):
    ch = SSM_CHUNK
    rows = slice(sub * ch, (sub + 1) * ch)
    a_cs = acs_ref[0, rows, :]
    a_src_t = asrc_ref[0, sub]
    w_bf = wt_ref[0, sub].astype(BF16)
    row_i = lax.broadcasted_iota(jnp.int32, (ch, ch), 0)
    col_i = lax.broadcasted_iota(jnp.int32, (ch, ch), 1)
    causal = row_i >= col_i
    e = jnp.exp2(a_cs)
    a_exp_all = jnp.dot(e.astype(BF16), spread_ref[...], preferred_element_type=F32)
    e_end = jnp.broadcast_to(e[ch - 1:ch, :], (8, LANES))
    e_end_hi = e_end.astype(BF16)
    e_end_lo = (e_end - e_end_hi.astype(F32)).astype(BF16)
    chunk_decay = (jnp.dot(e_end_hi, spread_ref[...], preferred_element_type=F32)
                   + jnp.dot(e_end_lo, spread_ref[...], preferred_element_type=F32))[0:1]

    for g in range(SSM_GROUPS):
        gs = slice(g * GROUP_WIDTH, (g + 1) * GROUP_WIDTH)
        b_bf = xc_ref[0, rows, SSM_INNER + g * SSM_STATE:SSM_INNER + (g + 1) * SSM_STATE]
        c_bf = xc_ref[0, rows, SSM_INNER + (SSM_GROUPS + g) * SSM_STATE:
                      SSM_INNER + (SSM_GROUPS + g + 1) * SSM_STATE]
        cb = lax.dot_general(c_bf, b_bf, (((1,), (1,)), ((), ())),
                             preferred_element_type=F32).astype(BF16)
        b_t = b_bf.astype(F32).T.astype(BF16)
        x_bf = xc_ref[0, rows, gs]

        intra, to_state, x_diag = [], [], []
        for hh in range(HEADS_PER_GROUP):
            h = g * HEADS_PER_GROUP + hh
            a_col = jnp.broadcast_to(a_cs[:, h:h + 1], (ch, ch))
            decay_dt = jnp.exp2(jnp.where(causal, a_col - a_src_t[h:h + 1, :], NEG))
            intra.append(cb * decay_dt.astype(BF16))
            to_state.append(b_t * w_bf[h:h + 1, :])
            x_diag.append(x_bf * mask_ref[hh])
        lhs = jnp.concatenate([jnp.concatenate(intra, axis=1), jnp.concatenate(to_state, axis=1)], axis=0)
        r = jnp.dot(lhs, jnp.concatenate(x_diag, axis=0), preferred_element_type=F32)
        y_diag, st_inc = r[:ch], r[ch:]

        a_exp = a_exp_all[:, gs]
        st_old = st_ref[g]
        y_off = jnp.dot(c_bf, st_old.astype(BF16), preferred_element_type=F32) * a_exp
        st_ref[g] = st_old * chunk_decay[:, gs] + st_inc

        y = y_diag + y_off + dskip_ref[:, gs] * x_bf.astype(F32)
        y = y * zs_ref[0, rows, gs].astype(F32)
        y_ref[0, rows, gs] = (y * _rms_scale(y) * nw_ref[:, gs]).astype(BF16)


def _ssd(xc, zs, a_cs, a_src_t, w_t, d_skip_wide, norm_w):
    b, s, _ = xc.shape
    ch = SSM_CHUNK * SSD_SUB
    blk = lambda w: pl.BlockSpec((1, ch, w), lambda bi, c: (bi, c, 0))
    per_chunk = pl.BlockSpec((1, SSD_SUB, SSM_HEADS, SSM_CHUNK), lambda bi, c: (bi, c, 0, 0))
    spread = _head_lane_matrix(SSM_HEADS, LANES)
    masks = _group_head_masks()
    return pl.pallas_call(
        _ssd_kernel,
        grid=(b, s // ch),
        in_specs=[blk(CONV_DIM), blk(SSM_INNER), blk(LANES), per_chunk, per_chunk,
                  _const_spec(spread.shape), _const_spec(masks.shape),
                  _const_spec((1, SSM_INNER)), _const_spec((1, SSM_INNER))],
        out_specs=blk(SSM_INNER),
        out_shape=jax.ShapeDtypeStruct((b, s, SSM_INNER), BF16),
        scratch_shapes=[pltpu.VMEM((SSM_GROUPS, SSM_STATE, GROUP_WIDTH), F32)],
        compiler_params=_params(2),
        name="ssd",
    )(xc, zs, a_cs, a_src_t, w_t, spread, masks, d_skip_wide, norm_w)


_ATT_SLABS = ATT_WIDTH // LANES


def _merge_kernel(x_ref, o1_ref, o4_ref, o16_ref, s1_ref, s4_ref, s16_ref, y_ref, gate_ref,
                  expand_ref, watt_ref, wssm_ref, wout_ref, nw_ref, pre_ref, h_ref, f_ref,
                  onat_ref, snat_ref, tmp_ref):
    tm = ROW_TILE
    ssm = jnp.dot(y_ref[...], wssm_ref[...], preferred_element_type=F32)
    for slot, (s_ref, o_ref, d) in enumerate(((s1_ref, o1_ref, 4), (s4_ref, o4_ref, 4),
                                              (s16_ref, o16_ref, 16))):
        for r in range(d):
            snat_ref[slot, pl.ds(r, tm // d, stride=d), :] = s_ref[0, r]
        for slab in range(_ATT_SLABS):
            cols = slice(slab * LANES, (slab + 1) * LANES)
            if d == 4:
                for r in range(d):
                    onat_ref[slot, slab, pl.ds(r, tm // d, stride=d), :] = o_ref[0, r, :, cols].astype(F32)
            else:
                for r in range(d):
                    tmp_ref[slab, r % 4, pl.ds(r // 4, tm // d, stride=4), :] = (
                        o_ref[0, r, :, cols].astype(F32))
                for r4 in range(4):
                    onat_ref[slot, slab, pl.ds(r4, tm // 4, stride=4), :] = tmp_ref[slab, r4]

    stats = [snat_ref[0], snat_ref[1], snat_ref[2]]
    top = jnp.maximum(jnp.maximum(stats[0], stats[1]), stats[2])
    scale = [jnp.exp2(st - top) for st in stats]
    dens = [pltpu.roll(st, LANES - DEN_LANE, axis=1) for st in stats]
    total = scale[0] * dens[0] + scale[1] * dens[1] + scale[2] * dens[2]
    is_head = lax.broadcasted_iota(jnp.int32, (tm, LANES), 1) < N_ATT_HEADS
    inv = 1.0 / jnp.where(is_head, total, 1.0)
    wts = [jnp.where(is_head, sc * inv, 0.0).astype(BF16) for sc in scale]
    slabs = []
    for slab in range(_ATT_SLABS):
        cols = slice(slab * LANES, (slab + 1) * LANES)
        wide = [jnp.dot(w, expand_ref[:, cols], preferred_element_type=F32) for w in wts]
        slabs.append(wide[0] * onat_ref[0, slab] + wide[1] * onat_ref[1, slab]
                     + wide[2] * onat_ref[2, slab])
    att = jnp.concatenate(slabs, axis=1).astype(BF16)
    att = jnp.dot(att, watt_ref[...], preferred_element_type=F32)
    gates = gate_ref[...].astype(F32)
    mixed = gates[:, :D_MODEL] * att + gates[:, D_MODEL:] * ssm
    mixed = jnp.dot(mixed.astype(BF16), wout_ref[...], preferred_element_type=F32)
    h = x_ref[...] + mixed * _rms_scale(mixed) * nw_ref[...]
    h_ref[...] = h
    f_ref[...] = (h * _rms_scale(h) * pre_ref[...]).astype(BF16)


def _merge(x2, outs, stats, y, gates, expand, w_att, w_ssm, w_out, norm_w, ffn_pre_w):
    t = x2.shape[0]
    tm = ROW_TILE
    row = lambda width: pl.BlockSpec((tm, width), lambda i: (i, 0))
    return pl.pallas_call(
        _merge_kernel,
        grid=(t // tm,),
        in_specs=[row(D_MODEL),
                  _residue4_spec(ATT_WIDTH), _residue4_spec(ATT_WIDTH), _residue16_spec(ATT_WIDTH),
                  _residue4_spec(LANES), _residue4_spec(LANES), _residue16_spec(LANES),
                  row(SSM_INNER), row(2 * D_MODEL),
                  _const_spec((LANES, ATT_WIDTH)), _const_spec((ATT_WIDTH, D_MODEL)),
                  _const_spec((SSM_INNER, D_MODEL)), _const_spec((D_MODEL, D_MODEL)),
                  _const_spec((1, D_MODEL)), _const_spec((1, D_MODEL))],
        out_specs=[row(D_MODEL), row(D_MODEL)],
        out_shape=[jax.ShapeDtypeStruct((t, D_MODEL), F32), jax.ShapeDtypeStruct((t, D_MODEL), BF16)],
        scratch_shapes=[pltpu.VMEM((3, _ATT_SLABS, tm, LANES), F32),
                        pltpu.VMEM((3, tm, LANES), F32),
                        pltpu.VMEM((_ATT_SLABS, 4, tm // 4, LANES), F32)],
        compiler_params=_params(1),
        name="merge",
    )(x2, *outs, *stats, y, gates, expand, w_att, w_ssm, w_out, norm_w, ffn_pre_w)


FFN_CHUNK = 1024
FFN_TILE = 1024


def _ffn_kernel(h_ref, f_ref, wup_ref, wdown_ref, post_ref, o_ref):
    acc = jnp.zeros((FFN_TILE, D_MODEL), F32)
    for c in range(0, FFN_HIDDEN, FFN_CHUNK):
        up = jnp.dot(f_ref[...], wup_ref[:, c:c + FFN_CHUNK], preferred_element_type=F32)
        act = jnp.square(jnp.maximum(up, 0.0)).astype(BF16)
        acc = acc + jnp.dot(act, wdown_ref[c:c + FFN_CHUNK, :], preferred_element_type=F32)
    o_ref[...] = h_ref[...] + acc * _rms_scale(acc) * post_ref[...]


def _ffn(h2, f2, w_up, w_down, post_w):
    t = h2.shape[0]
    row = pl.BlockSpec((FFN_TILE, D_MODEL), lambda i: (i, 0))
    return pl.pallas_call(
        _ffn_kernel,
        grid=(t // FFN_TILE,),
        in_specs=[row, row, _const_spec((D_MODEL, FFN_HIDDEN)),
                  _const_spec((FFN_HIDDEN, D_MODEL)), _const_spec((1, D_MODEL))],
        out_specs=row,
        out_shape=jax.ShapeDtypeStruct((t, D_MODEL), F32),
        compiler_params=_params(1),
        name="ffn",
    )(h2, f2, w_up, w_down, post_w)


def _pad_lanes(v):
    return jnp.pad(v.astype(F32), (0, LANES - v.shape[0])).reshape(1, LANES)


PREP_ROWS = 256


def _transpose_cast_kernel(wt_ref, o_ref, *, scale, first, total_rows):
    row = (first + pl.program_id(0)) * PREP_ROWS + lax.broadcasted_iota(jnp.int32, wt_ref.shape, 0)
    w = jnp.where(row < total_rows, wt_ref[...], 0.0)
    o_ref[...] = (w.T * scale).astype(BF16)


def _transpose_cast(wt, row0, n_rows, scale=1.0):
    k = wt.shape[1]
    first = row0 // PREP_ROWS
    assert row0 % PREP_ROWS == 0 and n_rows % PREP_ROWS == 0
    return pl.pallas_call(
        functools.partial(_transpose_cast_kernel, scale=scale, first=first, total_rows=wt.shape[0]),
        grid=(n_rows // PREP_ROWS,),
        in_specs=[pl.BlockSpec((PREP_ROWS, k), lambda j: (first + j, 0))],
        out_specs=pl.BlockSpec((k, PREP_ROWS), lambda j: (0, j)),
        out_shape=jax.ShapeDtypeStruct((k, n_rows), BF16),
        compiler_params=_params(1),
        name="weight_prep",
    )(wt)


def kernel(x, norm_mix_pre_w, w_in, b_gate, conv_w, conv_b, dt_bias, a_log, d_skip, ssm_norm_w,
           w_att_proj, w_ssm_proj, w_out, norm_mix_post_w, norm_ffn_pre_w, w_up, w_down,
           norm_ffn_post_w):
    b, s, d_model = x.shape
    depth = w_in.shape[0]
    assert d_model == D_MODEL and s % SPAN16 == 0
    t = b * s
    expand = _head_lane_matrix(N_ATT_HEADS, LANES)
    h = x.reshape(t, D_MODEL)
    for layer in range(depth):
        w_t_in = jnp.transpose(w_in[layer])
        off_ssm = 3 * ATT_WIDTH
        off_xbc = off_ssm + SSM_INNER
        off_dt = off_xbc + CONV_DIM
        w_qkv = _transpose_cast(w_t_in, 0, off_ssm)
        w_z_half = _transpose_cast(w_t_in, off_ssm, SSM_INNER, 0.5)
        w_xbc = _transpose_cast(w_t_in, off_xbc, CONV_DIM)
        n_tail = 2 * D_MODEL + SSM_HEADS
        w_tail = _transpose_cast(w_t_in, off_dt, -(-n_tail // PREP_ROWS) * PREP_ROWS)
        w_gate_half = 0.5 * w_tail[:, SSM_HEADS:n_tail]
        w_dt_pad = jnp.where(jnp.arange(LANES) < SSM_HEADS, w_tail[:, :LANES], 0).astype(BF16)
        qkv4, qkv16, u = _qkv_proj(h, norm_mix_pre_w[layer].reshape(1, -1), w_qkv)
        zs, xc, gates, a_cs, a_src_t, w_t = _ssm_proj(
            u, w_z_half, w_xbc, w_gate_half, w_dt_pad, 0.5 * b_gate[layer].reshape(1, -1),
            _pad_lanes(dt_bias[layer]),
            0.5 * conv_w[layer], 0.5 * conv_b[layer].reshape(1, -1), _pad_lanes(a_log[layer]), s)
        o1, s1, o4, s4 = _attention14(qkv4, b)
        o16, s16 = _attention(qkv16, 16, b)
        outs, stats = [o1, o4, o16], [s1, s4, s16]
        seq = lambda a: a.reshape(b, s, a.shape[-1])
        per_chunk = lambda a: a.reshape(b, s // SSM_CHUNK, SSM_HEADS, SSM_CHUNK)
        y = _ssd(seq(xc), seq(zs), seq(a_cs), per_chunk(a_src_t), per_chunk(w_t),
                 jnp.repeat(d_skip[layer].astype(F32), HEAD_DIM).reshape(1, -1),
                 ssm_norm_w[layer].reshape(1, -1))
        h, f = _merge(h, outs, stats, y.reshape(t, SSM_INNER), gates, expand,
                      w_att_proj[layer].astype(BF16), w_ssm_proj[layer].astype(BF16),
                      w_out[layer].astype(BF16), norm_mix_post_w[layer].reshape(1, -1),
                      norm_ffn_pre_w[layer].reshape(1, -1))
        h = _ffn(h, f, w_up[layer].astype(BF16), w_down[layer].astype(BF16),
                 norm_ffn_post_w[layer].reshape(1, -1))
    return h.reshape(b, s, D_MODEL)
```
